```python
import math
import jax, jax.numpy as jnp
from jax import lax
import numpy as np

D_MODEL = 1024
BATCH = 16
SEQ = 4096
DEPTH = 1

EPS = 1e-5
SSM_EXPAND = 2
D_INNER = SSM_EXPAND * D_MODEL
SSM_HEAD_DIM = 64
SSM_HEADS = D_INNER // SSM_HEAD_DIM
SSM_GROUPS = 4
D_STATE = 128
CONV_WIDTH = 4
CHUNK = 128
XBC_DIM = D_INNER + 2 * SSM_GROUPS * D_STATE
ATTN_HEADS = 8
ATTN_HEAD_DIM = 64
ATTN_V_DIM = 2 * ATTN_HEAD_DIM
ATTN_QK_DIM = ATTN_HEADS * 2 * ATTN_HEAD_DIM
ATTN_WIDTH = ATTN_HEADS * ATTN_V_DIM
ROPE_THETA = 10000.0
Q_BLOCK = 128
IN_SIZES = (D_INNER, XBC_DIM, SSM_HEADS, ATTN_QK_DIM, ATTN_QK_DIM, ATTN_WIDTH, ATTN_WIDTH, 2 * D_MODEL)
IN_DIM = sum(IN_SIZES)
SPLIT_POINTS = tuple(int(v) for v in np.cumsum(IN_SIZES)[:-1])

kernel_name = "hybrid_ssd_diffattn_gated_block"


def rmsnorm(x, w, eps=EPS):
    xf = x.astype(jnp.float32)
    y = xf * lax.rsqrt(jnp.mean(xf * xf, axis=-1, keepdims=True) + eps)
    return (y * w.astype(jnp.float32)).astype(x.dtype)


def rope_tables(seq, dim):
    inv = 1.0 / (ROPE_THETA ** (jnp.arange(0, dim, 2, dtype=jnp.float32) / dim))
    pos = jnp.arange(seq, dtype=jnp.float32)
    f = pos[:, None] * inv[None, :]
    emb = jnp.concatenate([f, f], axis=-1)
    return jnp.cos(emb), jnp.sin(emb)


def apply_rope(x, cos, sin):
    x1, x2 = jnp.split(x, 2, axis=-1)
    rot = jnp.concatenate([-x2, x1], axis=-1)
    c = cos[None, :, None, None, :]
    s = sin[None, :, None, None, :]
    return x * c + rot * s


def causal_depthwise_conv(u, w, bias):
    c = u.shape[-1]
    out = lax.conv_general_dilated(
        u, w.astype(u.dtype)[:, None, :], window_strides=(1,),
        padding=[(CONV_WIDTH - 1, 0)], dimension_numbers=('NWC', 'WIO', 'NWC'),
        feature_group_count=c)
    return out + bias.astype(u.dtype)


def ssd(xs, dt, a_log, bm, cm, d_skip):
    b, s = xs.shape[:2]
    nc = s // CHUNK
    r = SSM_HEADS // SSM_GROUPS
    a = -jnp.exp(a_log.astype(jnp.float32))
    adt = dt * a
    xdt = xs.astype(jnp.float32) * dt[..., None]
    x_c = xdt.reshape(b, nc, CHUNK, SSM_GROUPS, r, SSM_HEAD_DIM)
    b_c = bm.astype(jnp.float32).reshape(b, nc, CHUNK, SSM_GROUPS, D_STATE)
    c_c = cm.astype(jnp.float32).reshape(b, nc, CHUNK, SSM_GROUPS, D_STATE)
    adt_c = jnp.transpose(adt.reshape(b, nc, CHUNK, SSM_GROUPS, r), (0, 3, 4, 1, 2))
    a_cs = jnp.cumsum(adt_c, axis=-1)
    seg = a_cs[..., :, None] - a_cs[..., None, :]
    tril = jnp.tril(jnp.ones((CHUNK, CHUNK), dtype=bool))
    decay = jnp.exp(jnp.where(tril, seg, -jnp.inf))
    cb = jnp.einsum('bclgn,bcsgn->bcgls', c_c, b_c)
    y_diag = jnp.einsum('bcgls,bgrcls,bcsgrp->bclgrp', cb, decay, x_c)
    decay_states = jnp.exp(a_cs[..., -1:] - a_cs)
    states = jnp.einsum('bclgn,bgrcl,bclgrp->bcgrpn', b_c, decay_states, x_c)
    chunk_decay = jnp.exp(a_cs[..., -1])

    def step(carry, inp):
        st, dec = inp
        new = carry * dec[..., None, None] + st
        return new, carry

    init = jnp.zeros((b, SSM_GROUPS, r, SSM_HEAD_DIM, D_STATE), jnp.float32)
    _, prev = lax.scan(step, init, (jnp.moveaxis(states, 1, 0), jnp.moveaxis(chunk_decay, 3, 0)))
    prev = jnp.moveaxis(prev, 0, 1)
    y_off = jnp.einsum('bclgn,bcgrpn,bgrcl->bclgrp', c_c, prev, jnp.exp(a_cs))
    y = (y_diag + y_off).reshape(b, s, SSM_HEADS, SSM_HEAD_DIM)
    return y + xs.astype(jnp.float32) * d_skip.astype(jnp.float32)[:, None]


def diff_attention(q, k, v, lam):
    b, s = q.shape[:2]
    nq = s // Q_BLOCK
    q_blocks = jnp.moveaxis(q.reshape(b, nq, Q_BLOCK, ATTN_HEADS, 2, ATTN_HEAD_DIM), 1, 0)
    k_pos = jnp.arange(s)
    vf = v.astype(jnp.float32)

    def block(args):
        q_blk, idx = args
        scores = jnp.einsum('bqhmd,bkhmd->bhmqk', q_blk, k).astype(jnp.float32)
        q_pos = idx * Q_BLOCK + jnp.arange(Q_BLOCK)
        mask = k_pos[None, :] <= q_pos[:, None]
        p = jax.nn.softmax(jnp.where(mask, scores, -jnp.inf), axis=-1)
        w = p[:, :, 0] - lam * p[:, :, 1]
        return jnp.einsum('bhqk,bkhe->bqhe', w, vf)

    out = lax.map(block, (q_blocks, jnp.arange(nq)))
    return jnp.moveaxis(out, 0, 1).reshape(b, s, ATTN_HEADS, ATTN_V_DIM)


def setup_inputs(seed: int = 0) -> dict:
    key = jax.random.key(seed)
    ks = jax.random.split(key, 20)
    L = DEPTH
    nrm = jax.random.normal
    x = nrm(ks[0], (BATCH, SEQ, D_MODEL), jnp.float32)
    norm_w = 1.0 + 0.02 * nrm(ks[1], (L, D_MODEL), jnp.float32)
    w_in = nrm(ks[2], (L, D_MODEL, IN_DIM), jnp.float32) * D_MODEL ** -0.5
    conv_w = nrm(ks[3], (L, CONV_WIDTH, XBC_DIM), jnp.float32) * CONV_WIDTH ** -0.5
    conv_b = 0.01 * nrm(ks[4], (L, XBC_DIM), jnp.float32)
    u = jax.random.uniform(ks[5], (L, SSM_HEADS), jnp.float32)
    dt0 = jnp.maximum(jnp.exp(u * (math.log(0.1) - math.log(0.001)) + math.log(0.001)), 1e-4)
    dt_bias = dt0 + jnp.log(-jnp.expm1(-dt0))
    a_log = jnp.log(jax.random.uniform(ks[6], (L, SSM_HEADS), jnp.float32, 1.0, 16.0))
    d_skip = 1.0 + 0.1 * nrm(ks[7], (L, SSM_HEADS), jnp.float32)
    ssm_norm_w = 1.0 + 0.02 * nrm(ks[8], (L, D_INNER), jnp.float32)
    lambda_q1 = 0.1 * nrm(ks[9], (L, ATTN_HEAD_DIM), jnp.float32)
    lambda_k1 = 0.1 * nrm(ks[10], (L, ATTN_HEAD_DIM), jnp.float32)
    lambda_q2 = 0.1 * nrm(ks[11], (L, ATTN_HEAD_DIM), jnp.float32)
    lambda_k2 = 0.1 * nrm(ks[12], (L, ATTN_HEAD_DIM), jnp.float32)
    subln_w = 1.0 + 0.02 * nrm(ks[13], (L, ATTN_V_DIM), jnp.float32)
    gate_b = 0.01 * nrm(ks[14], (L, 2 * D_MODEL), jnp.float32)
    w_proj_ssm = nrm(ks[15], (L, D_INNER, D_MODEL), jnp.float32) * D_INNER ** -0.5
    w_proj_attn = nrm(ks[16], (L, ATTN_WIDTH, D_MODEL), jnp.float32) * ATTN_WIDTH ** -0.5
    w_out = nrm(ks[17], (L, D_MODEL, D_MODEL), jnp.float32) * D_MODEL ** -0.5
    final_norm_w = 1.0 + 0.02 * nrm(ks[18], (D_MODEL,), jnp.float32)
    return {"x": x, "norm_w": norm_w, "w_in": w_in, "conv_w": conv_w, "conv_b": conv_b,
            "dt_bias": dt_bias, "a_log": a_log, "d_skip": d_skip, "ssm_norm_w": ssm_norm_w,
            "lambda_q1": lambda_q1, "lambda_k1": lambda_k1, "lambda_q2": lambda_q2, "lambda_k2": lambda_k2,
            "subln_w": subln_w, "gate_b": gate_b, "w_proj_ssm": w_proj_ssm, "w_proj_attn": w_proj_attn,
            "w_out": w_out, "final_norm_w": final_norm_w}


def reference(x, norm_w, w_in, conv_w, conv_b, dt_bias, a_log, d_skip, ssm_norm_w,
              lambda_q1, lambda_k1, lambda_q2, lambda_k2, subln_w, gate_b,
              w_proj_ssm, w_proj_attn, w_out, final_norm_w):
    b, s, _ = x.shape
    cos, sin = rope_tables(s, ATTN_HEAD_DIM)
    for l in range(DEPTH):
        lambda_init = 0.8 - 0.6 * math.exp(-0.3 * l)
        h = rmsnorm(x, norm_w[l])
        proj = h @ w_in[l]
        z_ssm, xbc, dt_raw, q, k, v, z_attn, gate_logits = jnp.split(proj, SPLIT_POINTS, axis=-1)

        xbc = jax.nn.silu(causal_depthwise_conv(xbc, conv_w[l], conv_b[l]))
        xs, bm, cm = jnp.split(xbc, [D_INNER, D_INNER + SSM_GROUPS * D_STATE], axis=-1)
        dt = jax.nn.softplus(dt_raw.astype(jnp.float32) + dt_bias[l].astype(jnp.float32))
        y = ssd(xs.reshape(b, s, SSM_HEADS, SSM_HEAD_DIM), dt, a_log[l],
                bm.reshape(b, s, SSM_GROUPS, D_STATE), cm.reshape(b, s, SSM_GROUPS, D_STATE), d_skip[l])
        y = y.reshape(b, s, D_INNER) * jax.nn.silu(z_ssm.astype(jnp.float32))
        y_ssm = rmsnorm(y.reshape(b, s, SSM_GROUPS, D_INNER // SSM_GROUPS),
                        ssm_norm_w[l].reshape(SSM_GROUPS, D_INNER // SSM_GROUPS)).reshape(b, s, D_INNER)

        q = apply_rope(q.reshape(b, s, ATTN_HEADS, 2, ATTN_HEAD_DIM), cos, sin) * ATTN_HEAD_DIM ** -0.5
        k = apply_rope(k.reshape(b, s, ATTN_HEADS, 2, ATTN_HEAD_DIM), cos, sin)
        lam = (jnp.exp(jnp.sum(lambda_q1[l].astype(jnp.float32) * lambda_k1[l].astype(jnp.float32)))
               - jnp.exp(jnp.sum(lambda_q2[l].astype(jnp.float32) * lambda_k2[l].astype(jnp.float32)))
               + lambda_init)
        o = diff_attention(q, k, v.reshape(b, s, ATTN_HEADS, ATTN_V_DIM), lam)
        o = rmsnorm(o, subln_w[l]) * (1.0 - lambda_init)
        y_attn = o.reshape(b, s, ATTN_WIDTH) * jax.nn.silu(z_attn.astype(jnp.float32))

        gates = jax.nn.sigmoid(gate_logits.astype(jnp.float32) + gate_b[l].astype(jnp.float32))
        g_ssm, g_attn = jnp.split(gates, 2, axis=-1)
        merged = g_ssm * (y_ssm @ w_proj_ssm[l]) + g_attn * (y_attn @ w_proj_attn[l])
        x = x + (merged @ w_out[l]).astype(x.dtype)
    return rmsnorm(x, final_norm_w)
```

```python
import functools
import math

import jax
import jax.numpy as jnp
from jax import lax
from jax.experimental import pallas as pl
from jax.experimental.pallas import tpu as pltpu

F32 = jnp.float32
BF16 = jnp.bfloat16

D_MODEL = 1024
EPS = 1e-5
D_INNER = 2 * D_MODEL
SSM_HEAD_DIM = 64
SSM_HEADS = D_INNER // SSM_HEAD_DIM
SSM_GROUPS = 4
HEADS_PER_GROUP = SSM_HEADS // SSM_GROUPS
D_STATE = 128
CONV_WIDTH = 4
CHUNK = 128
BC_DIM = 2 * SSM_GROUPS * D_STATE
XBC_DIM = D_INNER + BC_DIM
ATTN_HEADS = 8
ATTN_HEAD_DIM = 64
ATTN_V_DIM = 2 * ATTN_HEAD_DIM
ATTN_WIDTH = ATTN_HEADS * ATTN_V_DIM
ROPE_THETA = 10000.0

LANES = 128
CARRY_ROWS = 8

COL_TILE = 1024
N_COL_TILES = 11
MAIN_DIM = COL_TILE * N_COL_TILES
Q_TILE = 5
K_TILE = 6
DT_PAD = LANES

VMEM_LIMIT = 48 * 1024 * 1024


def _silu(x):
    return x * (1.0 / (1.0 + jnp.exp(-x)))


def _dot(a, b):
    return jnp.dot(a, b, preferred_element_type=F32)


def _dot_nt(a, b):
    return lax.dot_general(a, b, (((1,), (1,)), ((), ())), preferred_element_type=F32)


def _inproj_kernel(x_ref, nw_ref, w_ref, wdt_hi_ref, wdt_lo_ref, cos_ref, sin_ref,
                   out_ref, dt_ref, h_ref):
    j = pl.program_id(1)

    @pl.when(j == 0)
    def _():
        x = x_ref[...]
        var = jnp.mean(x * x, axis=-1, keepdims=True)
        h = x * lax.rsqrt(var + EPS) * nw_ref[...]
        h_hi = h.astype(BF16)
        h_ref[...] = h_hi
        h_lo = (h - h_hi.astype(F32)).astype(BF16)
        dt_ref[...] = (_dot(h_hi, wdt_hi_ref[...]) + _dot(h_lo, wdt_hi_ref[...])
                       + _dot(h_hi, wdt_lo_ref[...]))

    acc = _dot(h_ref[...], w_ref[...])
    is_rope = (j == Q_TILE) | (j == K_TILE)

    @pl.when(is_rope)
    def _():
        scale = jnp.where(j == Q_TILE, ATTN_HEAD_DIM ** -0.5, 1.0).astype(F32)
        cos = cos_ref[...] * scale
        sin = sin_ref[...] * scale
        lane = lax.broadcasted_iota(jnp.int32, cos.shape, 1)
        first_half = (lane % ATTN_HEAD_DIM) < (ATTN_HEAD_DIM // 2)
        for c in range(COL_TILE // LANES):
            a = acc[:, c * LANES:(c + 1) * LANES]
            rot = jnp.where(first_half,
                            pltpu.roll(a, LANES - ATTN_HEAD_DIM // 2, 1),
                            pltpu.roll(a, ATTN_HEAD_DIM // 2, 1))
            out_ref[:, c * LANES:(c + 1) * LANES] = (a * cos + rot * sin).astype(out_ref.dtype)

    @pl.when(jnp.logical_not(is_rope))
    def _():
        out_ref[...] = acc.astype(out_ref.dtype)


def _inproj(x2, norm_w, w_main, wdt_hi, wdt_lo, cos_t, sin_t, seq):
    tokens = x2.shape[0]
    tm = min(1024, seq)
    seq_tiles = seq // tm
    return pl.pallas_call(
        _inproj_kernel,
        grid=(tokens // tm, N_COL_TILES),
        in_specs=[
            pl.BlockSpec((tm, D_MODEL), lambda i, j: (i, 0)),
            pl.BlockSpec((1, D_MODEL), lambda i, j: (0, 0)),
            pl.BlockSpec((D_MODEL, COL_TILE), lambda i, j: (0, j)),
            pl.BlockSpec((D_MODEL, DT_PAD), lambda i, j: (0, 0)),
            pl.BlockSpec((D_MODEL, DT_PAD), lambda i, j: (0, 0)),
            pl.BlockSpec((tm, LANES), lambda i, j: (i % seq_tiles, 0)),
            pl.BlockSpec((tm, LANES), lambda i, j: (i % seq_tiles, 0)),
        ],
        out_specs=[
            pl.BlockSpec((tm, COL_TILE), lambda i, j: (i, j)),
            pl.BlockSpec((tm, DT_PAD), lambda i, j: (i, 0)),
        ],
        out_shape=[
            jax.ShapeDtypeStruct((tokens, MAIN_DIM), BF16),
            jax.ShapeDtypeStruct((tokens, DT_PAD), F32),
        ],
        scratch_shapes=[pltpu.VMEM((tm, D_MODEL), BF16)],
        compiler_params=pltpu.CompilerParams(
            dimension_semantics=("parallel", "arbitrary"), vmem_limit_bytes=VMEM_LIMIT),
        name="inproj",
    )(x2, norm_w, w_main, wdt_hi, wdt_lo, cos_t, sin_t)


def _conv_silu(u, carry_ref, w_ref, b_ref):
    ext = jnp.concatenate([carry_ref[...], u], axis=0)
    out = b_ref[...] + w_ref[CONV_WIDTH - 1:CONV_WIDTH, :] * u
    for k in range(1, CONV_WIDTH):
        shifted = ext[CARRY_ROWS - k:CARRY_ROWS - k + CHUNK, :]
        out = out + w_ref[CONV_WIDTH - 1 - k:CONV_WIDTH - k, :] * shifted
    carry_ref[...] = u[CHUNK - CARRY_ROWS:, :]
    return _silu(out)


def _split3(v):
    hi = v.astype(BF16)
    r = v - hi.astype(F32)
    mid = r.astype(BF16)
    lo = (r - mid.astype(F32)).astype(BF16)
    return hi, mid, lo


def _ssd_kernel(xs_ref, z_ref, bc_ref, dt_ref, cwx_ref, cbx_ref, cwbc_ref, cbbc_ref,
                dtb_ref, alog_ref, dskip_ref, nw_ref, y_ref,
                state_ref, carry_x_ref, carry_bc_ref, ybuf_ref):
    @pl.when(pl.program_id(1) == 0)
    def _():
        state_ref[...] = jnp.zeros_like(state_ref)
        carry_x_ref[...] = jnp.zeros_like(carry_x_ref)
        carry_bc_ref[...] = jnp.zeros_like(carry_bc_ref)

    xs = _conv_silu(xs_ref[...].astype(F32), carry_x_ref, cwx_ref, cbx_ref)
    bc = _conv_silu(bc_ref[...].astype(F32), carry_bc_ref, cwbc_ref, cbbc_ref)

    dt_in = dt_ref[...] + dtb_ref[...]
    dt = jnp.maximum(dt_in, 0.0) + jnp.log1p(jnp.exp(-jnp.abs(dt_in)))
    adt = dt * (-jnp.exp(alog_ref[...]))

    row = lax.broadcasted_iota(jnp.int32, (CHUNK, CHUNK), 0)
    col = lax.broadcasted_iota(jnp.int32, (CHUNK, CHUNK), 1)
    tril = row >= col
    tril_bf = tril.astype(BF16)
    hi, mid, lo = _split3(adt)
    a_cs = _dot(tril_bf, hi) + _dot(tril_bf, mid) + _dot(tril_bf, lo)
    a_cs_t = a_cs.T
    dt_t = dt.T
    a_last_col = a_cs_t[:, CHUNK - 1:CHUNK]
    w_t = dt_t * jnp.exp(a_last_col - a_cs_t)
    chunk_decay = jnp.exp(a_cs[CHUNK - 1:CHUNK, :])

    for g in range(SSM_GROUPS):
        b_g = bc[:, g * D_STATE:(g + 1) * D_STATE]
        c_g = bc[:, BC_DIM // 2 + g * D_STATE:BC_DIM // 2 + (g + 1) * D_STATE]
        cb = _dot_nt(c_g.astype(BF16), b_g.astype(BF16))
        b_g_t = b_g.T
        for r in range(HEADS_PER_GROUP):
            h = g * HEADS_PER_GROUP + r
            col_b = jnp.broadcast_to(a_cs[:, h:h + 1], (CHUNK, CHUNK))
            seg = col_b - a_cs_t[h:h + 1, :]
            decay = jnp.exp(jnp.where(tril, seg, -jnp.inf))
            m = (cb * decay * dt_t[h:h + 1, :]).astype(BF16)
            x_h = xs[:, h * SSM_HEAD_DIM:(h + 1) * SSM_HEAD_DIM].astype(BF16)
            s_h = state_ref[h]
            c_scaled = (c_g * jnp.exp(col_b)).astype(BF16)
            y_h = _dot(m, x_h) + _dot(c_scaled, s_h.astype(BF16))
            ybuf_ref[:, h * SSM_HEAD_DIM:(h + 1) * SSM_HEAD_DIM] = y_h
            b_w = (b_g_t * w_t[h:h + 1, :]).astype(BF16)
            state_ref[h] = s_h * chunk_decay[:, h:h + 1] + _dot(b_w, x_h)

    y = ybuf_ref[...] + xs * dskip_ref[...]
    y = y * _silu(z_ref[...].astype(F32))
    gw = D_INNER // SSM_GROUPS
    for g in range(SSM_GROUPS):
        yg = y[:, g * gw:(g + 1) * gw]
        var = jnp.mean(yg * yg, axis=-1, keepdims=True)
        y_ref[:, g * gw:(g + 1) * gw] = (
            yg * lax.rsqrt(var + EPS) * nw_ref[:, g * gw:(g + 1) * gw]).astype(y_ref.dtype)


def _ssd(main, dt_raw, cwx, cbx, cwbc, cbbc, dtb, alog, dskip, nw, batch, seq):
    nc = seq // CHUNK
    tokens = batch * seq
    row_idx = lambda b, c: b * nc + c
    const = lambda b, c: (0, 0)
    return pl.pallas_call(
        _ssd_kernel,
        grid=(batch, nc),
        in_specs=[
            pl.BlockSpec((CHUNK, D_INNER), lambda b, c: (row_idx(b, c), 0)),
            pl.BlockSpec((CHUNK, D_INNER), lambda b, c: (row_idx(b, c), 1)),
            pl.BlockSpec((CHUNK, BC_DIM), lambda b, c: (row_idx(b, c), 4)),
            pl.BlockSpec((CHUNK, DT_PAD), lambda b, c: (row_idx(b, c), 0)),
            pl.BlockSpec((CONV_WIDTH, D_INNER), const),
            pl.BlockSpec((1, D_INNER), const),
            pl.BlockSpec((CONV_WIDTH, BC_DIM), const),
            pl.BlockSpec((1, BC_DIM), const),
            pl.BlockSpec((1, DT_PAD), const),
            pl.BlockSpec((1, DT_PAD), const),
            pl.BlockSpec((1, D_INNER), const),
            pl.BlockSpec((1, D_INNER), const),
        ],
        out_specs=pl.BlockSpec((CHUNK, D_INNER), lambda b, c: (row_idx(b, c), 0)),
        out_shape=jax.ShapeDtypeStruct((tokens, D_INNER), BF16),
        scratch_shapes=[
            pltpu.VMEM((SSM_HEADS, D_STATE, SSM_HEAD_DIM), F32),
            pltpu.VMEM((CARRY_ROWS, D_INNER), F32),
            pltpu.VMEM((CARRY_ROWS, BC_DIM), F32),
            pltpu.VMEM((CHUNK, D_INNER), F32),
        ],
        compiler_params=pltpu.CompilerParams(
            dimension_semantics=("parallel", "arbitrary"), vmem_limit_bytes=VMEM_LIMIT),
        name="ssd",
    )(main, main, main, dt_raw, cwx, cbx, cwbc, cbbc, dtb, alog, dskip, nw)


def _attn_kernel(q_ref, k_ref, v_ref, z_ref, lq1_ref, lk1_ref, lq2_ref, lk2_ref, sw_ref,
                 o_ref, m_ref, l_ref, acc_ref, *, tq, lambda_init):
    qi = pl.program_id(2)
    q = q_ref[...]
    lane = lax.broadcasted_iota(jnp.int32, q.shape, 1)
    zero = jnp.zeros_like(q)
    qq = jnp.concatenate([jnp.where(lane < ATTN_HEAD_DIM, q, zero),
                          jnp.where(lane >= ATTN_HEAD_DIM, q, zero)], axis=0)

    m_ref[...] = jnp.full_like(m_ref, -jnp.inf)
    l_ref[...] = jnp.zeros_like(l_ref)
    acc_ref[...] = jnp.zeros_like(acc_ref)

    def step(j, masked):
        start = pl.multiple_of(j * tq, tq)
        kb = k_ref[pl.ds(start, tq), :]
        vb = v_ref[pl.ds(start, tq), :]
        s = _dot_nt(qq, kb)
        if masked:
            r = lax.broadcasted_iota(jnp.int32, s.shape, 0)
            c = lax.broadcasted_iota(jnp.int32, s.shape, 1)
            s = jnp.where(c <= (r & (tq - 1)), s, -jnp.inf)
        m_prev = m_ref[...]
        m_new = jnp.maximum(m_prev, jnp.max(s, axis=-1, keepdims=True))
        alpha = jnp.exp(m_prev - m_new)
        p = jnp.exp(s - m_new)
        l_ref[...] = alpha * l_ref[...] + jnp.sum(p, axis=-1, keepdims=True)
        acc_ref[...] = alpha * acc_ref[...] + _dot(p.astype(BF16), vb)
        m_ref[...] = m_new

    def body(j, carry):
        step(j, False)
        return carry

    lax.fori_loop(0, qi, body, 0)
    step(qi, True)

    lam = (jnp.exp(jnp.sum(lq1_ref[...] * lk1_ref[...], axis=-1, keepdims=True))
           - jnp.exp(jnp.sum(lq2_ref[...] * lk2_ref[...], axis=-1, keepdims=True))
           + lambda_init)
    o_maps = acc_ref[...] * (1.0 / l_ref[...])
    o = o_maps[:tq] - lam * o_maps[tq:]
    var = jnp.mean(o * o, axis=-1, keepdims=True)
    o = o * lax.rsqrt(var + EPS) * sw_ref[...] * (1.0 - lambda_init)
    o_ref[...] = (o * _silu(z_ref[...].astype(F32))).astype(o_ref.dtype)


def _attention(main, lq1, lk1, lq2, lk2, subln_w, batch, seq, lambda_init):
    tq = min(256, seq)
    assert tq & (tq - 1) == 0 and seq % tq == 0
    nq = seq // tq
    tokens = batch * seq
    q_col = Q_TILE * COL_TILE // ATTN_V_DIM
    k_col = K_TILE * COL_TILE // ATTN_V_DIM
    v_col = (K_TILE + 1) * COL_TILE // ATTN_V_DIM
    z_col = (K_TILE + 2) * COL_TILE // ATTN_V_DIM
    const = lambda b, h, i: (0, 0)
    return pl.pallas_call(
        functools.partial(_attn_kernel, tq=tq, lambda_init=lambda_init),
        grid=(batch, ATTN_HEADS, nq),
        in_specs=[
            pl.BlockSpec((tq, ATTN_V_DIM), lambda b, h, i: (b * nq + i, q_col + h)),
            pl.BlockSpec((seq, ATTN_V_DIM), lambda b, h, i: (b, k_col + h)),
            pl.BlockSpec((seq, ATTN_V_DIM), lambda b, h, i: (b, v_col + h)),
            pl.BlockSpec((tq, ATTN_V_DIM), lambda b, h, i: (b * nq + i, z_col + h)),
            pl.BlockSpec((1, ATTN_HEAD_DIM), const),
            pl.BlockSpec((1, ATTN_HEAD_DIM), const),
            pl.BlockSpec((1, ATTN_HEAD_DIM), const),
            pl.BlockSpec((1, ATTN_HEAD_DIM), const),
            pl.BlockSpec((1, ATTN_V_DIM), const),
        ],
        out_specs=pl.BlockSpec((tq, ATTN_V_DIM), lambda b, h, i: (b * nq + i, h)),
        out_shape=jax.ShapeDtypeStruct((tokens, ATTN_WIDTH), BF16),
        scratch_shapes=[
            pltpu.VMEM((2 * tq, 1), F32),
            pltpu.VMEM((2 * tq, 1), F32),
            pltpu.VMEM((2 * tq, ATTN_V_DIM), F32),
        ],
        compiler_params=pltpu.CompilerParams(
            dimension_semantics=("parallel", "parallel", "arbitrary"),
            vmem_limit_bytes=VMEM_LIMIT),
        name="diff_attn",
    )(main, main, main, main, lq1, lk1, lq2, lk2, subln_w)


def _merge_kernel(x_ref, ys_ref, ya_ref, gs_ref, ga_ref, gb_ref, ps_ref, pa_ref, wo_ref, fw_ref,
                  o_ref, *, final_norm):
    proj_s = _dot(ys_ref[...], ps_ref[...])
    proj_a = _dot(ya_ref[...], pa_ref[...])
    g_s = 1.0 / (1.0 + jnp.exp(-(gs_ref[...].astype(F32) + gb_ref[:, :D_MODEL])))
    g_a = 1.0 / (1.0 + jnp.exp(-(ga_ref[...].astype(F32) + gb_ref[:, D_MODEL:])))
    merged = g_s * proj_s + g_a * proj_a
    x = x_ref[...] + _dot(merged.astype(BF16), wo_ref[...])
    if final_norm:
        var = jnp.mean(x * x, axis=-1, keepdims=True)
        x = x * lax.rsqrt(var + EPS) * fw_ref[...]
    o_ref[...] = x


def _merge(x2, y_ssm, y_attn, main, gate_b, p_ssm, p_attn, w_out, final_w, final_norm):
    tokens = x2.shape[0]
    tm = min(512, tokens)
    const = lambda i: (0, 0)
    return pl.pallas_call(
        functools.partial(_merge_kernel, final_norm=final_norm),
        grid=(tokens // tm,),
        in_specs=[
            pl.BlockSpec((tm, D_MODEL), lambda i: (i, 0)),
            pl.BlockSpec((tm, D_INNER), lambda i: (i, 0)),
            pl.BlockSpec((tm, ATTN_WIDTH), lambda i: (i, 0)),
            pl.BlockSpec((tm, D_MODEL), lambda i: (i, N_COL_TILES - 2)),
            pl.BlockSpec((tm, D_MODEL), lambda i: (i, N_COL_TILES - 1)),
            pl.BlockSpec((1, 2 * D_MODEL), const),
            pl.BlockSpec((D_INNER, D_MODEL), const),
            pl.BlockSpec((ATTN_WIDTH, D_MODEL), const),
            pl.BlockSpec((D_MODEL, D_MODEL), const),
            pl.BlockSpec((1, D_MODEL), const),
        ],
        out_specs=pl.BlockSpec((tm, D_MODEL), lambda i: (i, 0)),
        out_shape=jax.ShapeDtypeStruct((tokens, D_MODEL), F32),
        compiler_params=pltpu.CompilerParams(
            dimension_semantics=("parallel",), vmem_limit_bytes=VMEM_LIMIT),
        name="merge",
    )(x2, y_ssm, y_attn, main, main, gate_b, p_ssm, p_attn, w_out, final_w)


def _rope_tables(seq):
    inv = 1.0 / (ROPE_THETA ** (jnp.arange(0, ATTN_HEAD_DIM, 2, dtype=F32) / ATTN_HEAD_DIM))
    f = jnp.arange(seq, dtype=F32)[:, None] * inv[None, :]
    cos = jnp.cos(f)
    sin = jnp.sin(f)
    cos_t = jnp.concatenate([cos, cos, cos, cos], axis=-1)
    sin_t = jnp.concatenate([-sin, sin, -sin, sin], axis=-1)
    return cos_t, sin_t


def kernel(x, norm_w, w_in, conv_w, conv_b, dt_bias, a_log, d_skip, ssm_norm_w,
           lambda_q1, lambda_k1, lambda_q2, lambda_k2, subln_w, gate_b,
           w_proj_ssm, w_proj_attn, w_out, final_norm_w):
    batch, seq, _ = x.shape
    depth = norm_w.shape[0]
    assert seq % CHUNK == 0
    cos_t, sin_t = _rope_tables(seq)
    x2 = x.reshape(batch * seq, D_MODEL)
    pad_heads = lambda v: jnp.pad(v.astype(F32), (0, DT_PAD - SSM_HEADS)).reshape(1, DT_PAD)

    dt_off = D_INNER + XBC_DIM
    for l in range(depth):
        lambda_init = 0.8 - 0.6 * math.exp(-0.3 * l)
        w = w_in[l]
        w_main = jnp.concatenate(
            [w[:, D_INNER:2 * D_INNER], w[:, :D_INNER], w[:, 2 * D_INNER:dt_off],
             w[:, dt_off + SSM_HEADS:]], axis=1).astype(BF16)
        w_dt = jnp.pad(w[:, dt_off:dt_off + SSM_HEADS], ((0, 0), (0, DT_PAD - SSM_HEADS)))
        wdt_hi = w_dt.astype(BF16)
        wdt_lo = (w_dt - wdt_hi.astype(F32)).astype(BF16)

        main, dt_raw = _inproj(x2, norm_w[l].reshape(1, D_MODEL), w_main, wdt_hi, wdt_lo,
                               cos_t, sin_t, seq)

        y_ssm = _ssd(
            main, dt_raw,
            conv_w[l][:, :D_INNER], conv_b[l][:D_INNER].reshape(1, D_INNER),
            conv_w[l][:, D_INNER:], conv_b[l][D_INNER:].reshape(1, BC_DIM),
            pad_heads(dt_bias[l]), pad_heads(a_log[l]),
            jnp.repeat(d_skip[l].astype(F32), SSM_HEAD_DIM).reshape(1, D_INNER),
            ssm_norm_w[l].reshape(1, D_INNER), batch, seq)

        y_attn = _attention(
            main, lambda_q1[l].reshape(1, -1), lambda_k1[l].reshape(1, -1),
            lambda_q2[l].reshape(1, -1), lambda_k2[l].reshape(1, -1),
            subln_w[l].reshape(1, ATTN_V_DIM), batch, seq, lambda_init)

        x2 = _merge(x2, y_ssm, y_attn, main, gate_b[l].reshape(1, 2 * D_MODEL),
                    w_proj_ssm[l].astype(BF16), w_proj_attn[l].astype(BF16),
                    w_out[l].astype(BF16), final_norm_w.reshape(1, D_MODEL),
                    final_norm=(l == depth - 1))
    return x2.reshape(batch, seq, D_MODEL)
```

```python
import functools
import math

import jax
import jax.numpy as jnp
from jax import lax
from jax.experimental import pallas as pl
from jax.experimental.pallas import tpu as pltpu

F32 = jnp.float32
BF16 = jnp.bfloat16

D_MODEL = 1024
EPS = 1e-5
D_INNER = 2 * D_MODEL
SSM_HEAD_DIM = 64
SSM_HEADS = D_INNER // SSM_HEAD_DIM
SSM_GROUPS = 4
HEADS_PER_GROUP = SSM_HEADS // SSM_GROUPS
D_STATE = 128
CONV_WIDTH = 4
CHUNK = 128
BC_DIM = 2 * SSM_GROUPS * D_STATE
XBC_DIM = D_INNER + BC_DIM
ATTN_HEADS = 8
ATTN_HEAD_DIM = 64
ATTN_V_DIM = 2 * ATTN_HEAD_DIM
ATTN_WIDTH = ATTN_HEADS * ATTN_V_DIM
ROPE_THETA = 10000.0
LOG2_E = math.log2(math.e)
ATTN_BLOCK = 512

LANES = 128
CARRY_ROWS = 8

COL_TILE = 1024
N_COL_TILES = 11
MAIN_DIM = COL_TILE * N_COL_TILES
Q_TILE = 5
K_TILE = 6
DT_PAD = LANES

VMEM_LIMIT = 48 * 1024 * 1024


def _silu(x):
    return x * (1.0 / (1.0 + jnp.exp(-x)))


def _dot(a, b):
    return jnp.dot(a, b, preferred_element_type=F32)


def _dot_nt(a, b):
    return lax.dot_general(a, b, (((1,), (1,)), ((), ())), preferred_element_type=F32)


def _inproj_kernel(x_ref, nw_ref, w_ref, wdt_hi_ref, wdt_lo_ref, cos_ref, sin_ref,
                   out_ref, dt_ref, h_ref):
    j = pl.program_id(1)

    @pl.when(j == 0)
    def _():
        x = x_ref[...]
        var = jnp.mean(x * x, axis=-1, keepdims=True)
        h = x * lax.rsqrt(var + EPS) * nw_ref[...]
        h_hi = h.astype(BF16)
        h_ref[...] = h_hi
        h_lo = (h - h_hi.astype(F32)).astype(BF16)
        dt_ref[...] = (_dot(h_hi, wdt_hi_ref[...]) + _dot(h_lo, wdt_hi_ref[...])
                       + _dot(h_hi, wdt_lo_ref[...]))

    acc = _dot(h_ref[...], w_ref[...])
    is_rope = (j == Q_TILE) | (j == K_TILE)

    @pl.when(is_rope)
    def _():
        scale = jnp.where(j == Q_TILE, ATTN_HEAD_DIM ** -0.5 * LOG2_E, 1.0).astype(F32)
        cos = cos_ref[...] * scale
        sin = sin_ref[...] * scale
        lane = lax.broadcasted_iota(jnp.int32, cos.shape, 1)
        first_half = (lane % ATTN_HEAD_DIM) < (ATTN_HEAD_DIM // 2)
        for c in range(COL_TILE // LANES):
            a = acc[:, c * LANES:(c + 1) * LANES]
            rot = jnp.where(first_half,
                            pltpu.roll(a, LANES - ATTN_HEAD_DIM // 2, 1),
                            pltpu.roll(a, ATTN_HEAD_DIM // 2, 1))
            out_ref[:, c * LANES:(c + 1) * LANES] = (a * cos + rot * sin).astype(out_ref.dtype)

    @pl.when(jnp.logical_not(is_rope))
    def _():
        out_ref[...] = acc.astype(out_ref.dtype)


def _inproj(x2, norm_w, w_main, wdt_hi, wdt_lo, cos_t, sin_t, seq):
    tokens = x2.shape[0]
    tm = min(1024, seq)
    seq_tiles = seq // tm
    return pl.pallas_call(
        _inproj_kernel,
        grid=(tokens // tm, N_COL_TILES),
        in_specs=[
            pl.BlockSpec((tm, D_MODEL), lambda i, j: (i, 0)),
            pl.BlockSpec((1, D_MODEL), lambda i, j: (0, 0)),
            pl.BlockSpec((D_MODEL, COL_TILE), lambda i, j: (0, j)),
            pl.BlockSpec((D_MODEL, DT_PAD), lambda i, j: (0, 0)),
            pl.BlockSpec((D_MODEL, DT_PAD), lambda i, j: (0, 0)),
            pl.BlockSpec((tm, LANES), lambda i, j: (i % seq_tiles, 0)),
            pl.BlockSpec((tm, LANES), lambda i, j: (i % seq_tiles, 0)),
        ],
        out_specs=[
            pl.BlockSpec((tm, COL_TILE), lambda i, j: (i, j)),
            pl.BlockSpec((tm, DT_PAD), lambda i, j: (i, 0)),
        ],
        out_shape=[
            jax.ShapeDtypeStruct((tokens, MAIN_DIM), BF16),
            jax.ShapeDtypeStruct((tokens, DT_PAD), F32),
        ],
        scratch_shapes=[pltpu.VMEM((tm, D_MODEL), BF16)],
        compiler_params=pltpu.CompilerParams(
            dimension_semantics=("parallel", "arbitrary"), vmem_limit_bytes=VMEM_LIMIT),
        name="inproj",
    )(x2, norm_w, w_main, wdt_hi, wdt_lo, cos_t, sin_t)


def _conv_silu(u, carry_ref, w_ref, b_ref):
    ext = jnp.concatenate([carry_ref[...], u], axis=0)
    out = b_ref[...] + w_ref[CONV_WIDTH - 1:CONV_WIDTH, :] * u
    for k in range(1, CONV_WIDTH):
        shifted = ext[CARRY_ROWS - k:CARRY_ROWS - k + CHUNK, :]
        out = out + w_ref[CONV_WIDTH - 1 - k:CONV_WIDTH - k, :] * shifted
    carry_ref[...] = u[CHUNK - CARRY_ROWS:, :]
    return _silu(out)


def _split3(v):
    hi = v.astype(BF16)
    r = v - hi.astype(F32)
    mid = r.astype(BF16)
    lo = (r - mid.astype(F32)).astype(BF16)
    return hi, mid, lo


def _ssd_kernel(xs_ref, z_ref, bc_ref, dt_ref, cwx_ref, cbx_ref, cwbc_ref, cbbc_ref,
                dtb_ref, alog_ref, dskip_ref, nw_ref, y_ref,
                state_ref, carry_x_ref, carry_bc_ref, ybuf_ref):
    @pl.when(pl.program_id(1) == 0)
    def _():
        state_ref[...] = jnp.zeros_like(state_ref)
        carry_x_ref[...] = jnp.zeros_like(carry_x_ref)
        carry_bc_ref[...] = jnp.zeros_like(carry_bc_ref)

    xs = _conv_silu(xs_ref[...].astype(F32), carry_x_ref, cwx_ref, cbx_ref)
    bc = _conv_silu(bc_ref[...].astype(F32), carry_bc_ref, cwbc_ref, cbbc_ref)

    dt_in = dt_ref[...] + dtb_ref[...]
    dt = jnp.maximum(dt_in, 0.0) + jnp.log1p(jnp.exp(-jnp.abs(dt_in)))
    adt = dt * (-jnp.exp(alog_ref[...]))

    row = lax.broadcasted_iota(jnp.int32, (CHUNK, CHUNK), 0)
    col = lax.broadcasted_iota(jnp.int32, (CHUNK, CHUNK), 1)
    tril = row >= col
    tril_bf = tril.astype(BF16)
    hi, mid, lo = _split3(adt)
    a_cs = _dot(tril_bf, hi) + _dot(tril_bf, mid) + _dot(tril_bf, lo)
    a_cs_t = a_cs.T
    dt_t = dt.T
    a_last_col = a_cs_t[:, CHUNK - 1:CHUNK]
    w_t = dt_t * jnp.exp(a_last_col - a_cs_t)
    chunk_decay = jnp.exp(a_cs[CHUNK - 1:CHUNK, :])

    for g in range(SSM_GROUPS):
        b_g = bc[:, g * D_STATE:(g + 1) * D_STATE]
        c_g = bc[:, BC_DIM // 2 + g * D_STATE:BC_DIM // 2 + (g + 1) * D_STATE]
        cb = _dot_nt(c_g.astype(BF16), b_g.astype(BF16))
        b_g_t = b_g.T
        for r in range(HEADS_PER_GROUP):
            h = g * HEADS_PER_GROUP + r
            col_b = jnp.broadcast_to(a_cs[:, h:h + 1], (CHUNK, CHUNK))
            seg = col_b - a_cs_t[h:h + 1, :]
            decay = jnp.exp(jnp.where(tril, seg, -jnp.inf))
            m = (cb * decay * dt_t[h:h + 1, :]).astype(BF16)
            x_h = xs[:, h * SSM_HEAD_DIM:(h + 1) * SSM_HEAD_DIM].astype(BF16)
            s_h = state_ref[h]
            c_scaled = (c_g * jnp.exp(col_b)).astype(BF16)
            y_h = _dot(m, x_h) + _dot(c_scaled, s_h.astype(BF16))
            ybuf_ref[:, h * SSM_HEAD_DIM:(h + 1) * SSM_HEAD_DIM] = y_h
            b_w = (b_g_t * w_t[h:h + 1, :]).astype(BF16)
            state_ref[h] = s_h * chunk_decay[:, h:h + 1] + _dot(b_w, x_h)

    y = ybuf_ref[...] + xs * dskip_ref[...]
    y = y * _silu(z_ref[...].astype(F32))
    gw = D_INNER // SSM_GROUPS
    for g in range(SSM_GROUPS):
        yg = y[:, g * gw:(g + 1) * gw]
        var = jnp.mean(yg * yg, axis=-1, keepdims=True)
        y_ref[:, g * gw:(g + 1) * gw] = (
            yg * lax.rsqrt(var + EPS) * nw_ref[:, g * gw:(g + 1) * gw]).astype(y_ref.dtype)


def _ssd(main, dt_raw, cwx, cbx, cwbc, cbbc, dtb, alog, dskip, nw, batch, seq):
    nc = seq // CHUNK
    tokens = batch * seq
    row_idx = lambda b, c: b * nc + c
    const = lambda b, c: (0, 0)
    return pl.pallas_call(
        _ssd_kernel,
        grid=(batch, nc),
        in_specs=[
            pl.BlockSpec((CHUNK, D_INNER), lambda b, c: (row_idx(b, c), 0)),
            pl.BlockSpec((CHUNK, D_INNER), lambda b, c: (row_idx(b, c), 1)),
            pl.BlockSpec((CHUNK, BC_DIM), lambda b, c: (row_idx(b, c), 4)),
            pl.BlockSpec((CHUNK, DT_PAD), lambda b, c: (row_idx(b, c), 0)),
            pl.BlockSpec((CONV_WIDTH, D_INNER), const),
            pl.BlockSpec((1, D_INNER), const),
            pl.BlockSpec((CONV_WIDTH, BC_DIM), const),
            pl.BlockSpec((1, BC_DIM), const),
            pl.BlockSpec((1, DT_PAD), const),
            pl.BlockSpec((1, DT_PAD), const),
            pl.BlockSpec((1, D_INNER), const),
            pl.BlockSpec((1, D_INNER), const),
        ],
        out_specs=pl.BlockSpec((CHUNK, D_INNER), lambda b, c: (row_idx(b, c), 0)),
        out_shape=jax.ShapeDtypeStruct((tokens, D_INNER), BF16),
        scratch_shapes=[
            pltpu.VMEM((SSM_HEADS, D_STATE, SSM_HEAD_DIM), F32),
            pltpu.VMEM((CARRY_ROWS, D_INNER), F32),
            pltpu.VMEM((CARRY_ROWS, BC_DIM), F32),
            pltpu.VMEM((CHUNK, D_INNER), F32),
        ],
        compiler_params=pltpu.CompilerParams(
            dimension_semantics=("parallel", "arbitrary"), vmem_limit_bytes=VMEM_LIMIT),
        name="ssd",
    )(main, main, main, dt_raw, cwx, cbx, cwbc, cbbc, dtb, alog, dskip, nw)


def _attn_kernel(q_ref, k_ref, v_ref, z_ref, lq1_ref, lk1_ref, lq2_ref, lk2_ref, sw_ref,
                 o_ref, vt_ref, m_ref, l_ref, acc_ref, *, tq, lambda_init):
    qi = pl.program_id(2)
    nkv = vt_ref.shape[0]

    @pl.when(qi == 0)
    def _():
        for c in range(nkv):
            vt_ref[c] = v_ref[c * tq:(c + 1) * tq, :].astype(F32).T.astype(BF16)

    q_t = q_ref[...].astype(F32).T
    row = lax.broadcasted_iota(jnp.int32, q_t.shape, 0)
    qq_t = jnp.concatenate([jnp.where(row < ATTN_HEAD_DIM, q_t, 0.0),
                            jnp.where(row >= ATTN_HEAD_DIM, q_t, 0.0)], axis=1).astype(BF16)

    m_ref[...] = jnp.full_like(m_ref, -jnp.inf)
    l_ref[...] = jnp.zeros_like(l_ref)
    acc_ref[...] = jnp.zeros_like(acc_ref)

    def step(j, masked):
        start = pl.multiple_of(j * tq, tq)
        s = _dot(k_ref[pl.ds(start, tq), :], qq_t)
        if masked:
            r = lax.broadcasted_iota(jnp.int32, s.shape, 0)
            c = lax.broadcasted_iota(jnp.int32, s.shape, 1)
            s = jnp.where(r <= (c & (tq - 1)), s, -jnp.inf)
        m_prev = m_ref[...]
        m_new = jnp.maximum(m_prev, jnp.max(s, axis=0, keepdims=True))
        alpha = jnp.exp2(m_prev - m_new)
        p = jnp.exp2(s - m_new)
        l_ref[...] = alpha * l_ref[...] + jnp.sum(p, axis=0, keepdims=True)
        acc_ref[...] = alpha * acc_ref[...] + _dot(vt_ref[j], p.astype(BF16))
        m_ref[...] = m_new

    def body(j, carry):
        step(j, False)
        return carry

    lax.fori_loop(0, qi, body, 0)
    step(qi, True)

    lam = (jnp.exp(jnp.sum(lq1_ref[...] * lk1_ref[...], axis=-1, keepdims=True))
           - jnp.exp(jnp.sum(lq2_ref[...] * lk2_ref[...], axis=-1, keepdims=True))
           + lambda_init)
    o_maps = acc_ref[...] * (1.0 / l_ref[...])
    o = o_maps[:, :tq] - lam * o_maps[:, tq:]
    var = jnp.mean(o * o, axis=0, keepdims=True)
    o = o * lax.rsqrt(var + EPS) * sw_ref[...] * (1.0 - lambda_init)
    o_ref[...] = (o.T * _silu(z_ref[...].astype(F32))).astype(o_ref.dtype)


def _attention(main, lq1, lk1, lq2, lk2, subln_w, batch, seq, lambda_init):
    tq = min(ATTN_BLOCK, seq)
    assert tq & (tq - 1) == 0 and seq % tq == 0
    nq = seq // tq
    tokens = batch * seq
    q_col = Q_TILE * COL_TILE // ATTN_V_DIM
    k_col = K_TILE * COL_TILE // ATTN_V_DIM
    v_col = (K_TILE + 1) * COL_TILE // ATTN_V_DIM
    z_col = (K_TILE + 2) * COL_TILE // ATTN_V_DIM
    const = lambda b, h, i: (0, 0)
    return pl.pallas_call(
        functools.partial(_attn_kernel, tq=tq, lambda_init=lambda_init),
        grid=(batch, ATTN_HEADS, nq),
        in_specs=[
            pl.BlockSpec((tq, ATTN_V_DIM), lambda b, h, i: (b * nq + i, q_col + h)),
            pl.BlockSpec((seq, ATTN_V_DIM), lambda b, h, i: (b, k_col + h)),
            pl.BlockSpec((seq, ATTN_V_DIM), lambda b, h, i: (b, v_col + h)),
            pl.BlockSpec((tq, ATTN_V_DIM), lambda b, h, i: (b * nq + i, z_col + h)),
            pl.BlockSpec((1, ATTN_HEAD_DIM), const),
            pl.BlockSpec((1, ATTN_HEAD_DIM), const),
            pl.BlockSpec((1, ATTN_HEAD_DIM), const),
            pl.BlockSpec((1, ATTN_HEAD_DIM), const),
            pl.BlockSpec((ATTN_V_DIM, 1), const),
        ],
        out_specs=pl.BlockSpec((tq, ATTN_V_DIM), lambda b, h, i: (b * nq + i, h)),
        out_shape=jax.ShapeDtypeStruct((tokens, ATTN_WIDTH), BF16),
        scratch_shapes=[
            pltpu.VMEM((nq, ATTN_V_DIM, tq), BF16),
            pltpu.VMEM((1, 2 * tq), F32),
            pltpu.VMEM((1, 2 * tq), F32),
            pltpu.VMEM((ATTN_V_DIM, 2 * tq), F32),
        ],
        compiler_params=pltpu.CompilerParams(
            dimension_semantics=("parallel", "parallel", "arbitrary"),
            vmem_limit_bytes=VMEM_LIMIT),
        name="diff_attn",
    )(main, main, main, main, lq1, lk1, lq2, lk2, subln_w)


def _merge_kernel(x_ref, ys_ref, ya_ref, gs_ref, ga_ref, gb_ref, ps_ref, pa_ref, wo_ref, fw_ref,
                  o_ref, *, final_norm):
    proj_s = _dot(ys_ref[...], ps_ref[...])
    proj_a = _dot(ya_ref[...], pa_ref[...])
    g_s = 1.0 / (1.0 + jnp.exp(-(gs_ref[...].astype(F32) + gb_ref[:, :D_MODEL])))
    g_a = 1.0 / (1.0 + jnp.exp(-(ga_ref[...].astype(F32) + gb_ref[:, D_MODEL:])))
    merged = g_s * proj_s + g_a * proj_a
    x = x_ref[...] + _dot(merged.astype(BF16), wo_ref[...])
    if final_norm:
        var = jnp.mean(x * x, axis=-1, keepdims=True)
        x = x * lax.rsqrt(var + EPS) * fw_ref[...]
    o_ref[...] = x


def _merge(x2, y_ssm, y_attn, main, gate_b, p_ssm, p_attn, w_out, final_w, final_norm):
    tokens = x2.shape[0]
    tm = min(512, tokens)
    const = lambda i: (0, 0)
    return pl.pallas_call(
        functools.partial(_merge_kernel, final_norm=final_norm),
        grid=(tokens // tm,),
        in_specs=[
            pl.BlockSpec((tm, D_MODEL), lambda i: (i, 0)),
            pl.BlockSpec((tm, D_INNER), lambda i: (i, 0)),
            pl.BlockSpec((tm, ATTN_WIDTH), lambda i: (i, 0)),
            pl.BlockSpec((tm, D_MODEL), lambda i: (i, N_COL_TILES - 2)),
            pl.BlockSpec((tm, D_MODEL), lambda i: (i, N_COL_TILES - 1)),
            pl.BlockSpec((1, 2 * D_MODEL), const),
            pl.BlockSpec((D_INNER, D_MODEL), const),
            pl.BlockSpec((ATTN_WIDTH, D_MODEL), const),
            pl.BlockSpec((D_MODEL, D_MODEL), const),
            pl.BlockSpec((1, D_MODEL), const),
        ],
        out_specs=pl.BlockSpec((tm, D_MODEL), lambda i: (i, 0)),
        out_shape=jax.ShapeDtypeStruct((tokens, D_MODEL), F32),
        compiler_params=pltpu.CompilerParams(
            dimension_semantics=("parallel",), vmem_limit_bytes=VMEM_LIMIT),
        name="merge",
    )(x2, y_ssm, y_attn, main, main, gate_b, p_ssm, p_attn, w_out, final_w)


def _rope_tables(seq):
    inv = 1.0 / (ROPE_THETA ** (jnp.arange(0, ATTN_HEAD_DIM, 2, dtype=F32) / ATTN_HEAD_DIM))
    f = jnp.arange(seq, dtype=F32)[:, None] * inv[None, :]
    cos = jnp.cos(f)
    sin = jnp.sin(f)
    cos_t = jnp.concatenate([cos, cos, cos, cos], axis=-1)
    sin_t = jnp.concatenate([-sin, sin, -sin, sin], axis=-1)
    return cos_t, sin_t


def kernel(x, norm_w, w_in, conv_w, conv_b, dt_bias, a_log, d_skip, ssm_norm_w,
           lambda_q1, lambda_k1, lambda_q2, lambda_k2, subln_w, gate_b,
           w_proj_ssm, w_proj_attn, w_out, final_norm_w):
    batch, seq, _ = x.shape
    depth = norm_w.shape[0]
    assert seq % CHUNK == 0
    cos_t, sin_t = _rope_tables(seq)
    x2 = x.reshape(batch * seq, D_MODEL)
    pad_heads = lambda v: jnp.pad(v.astype(F32), (0, DT_PAD - SSM_HEADS)).reshape(1, DT_PAD)

    dt_off = D_INNER + XBC_DIM
    for l in range(depth):
        lambda_init = 0.8 - 0.6 * math.exp(-0.3 * l)
        w = w_in[l]
        w_main = jnp.concatenate(
            [w[:, D_INNER:2 * D_INNER], w[:, :D_INNER], w[:, 2 * D_INNER:dt_off],
             w[:, dt_off + SSM_HEADS:]], axis=1).astype(BF16)
        w_dt = jnp.pad(w[:, dt_off:dt_off + SSM_HEADS], ((0, 0), (0, DT_PAD - SSM_HEADS)))
        wdt_hi = w_dt.astype(BF16)
        wdt_lo = (w_dt - wdt_hi.astype(F32)).astype(BF16)

        main, dt_raw = _inproj(x2, norm_w[l].reshape(1, D_MODEL), w_main, wdt_hi, wdt_lo,
                               cos_t, sin_t, seq)

        y_ssm = _ssd(
            main, dt_raw,
            conv_w[l][:, :D_INNER], conv_b[l][:D_INNER].reshape(1, D_INNER),
            conv_w[l][:, D_INNER:], conv_b[l][D_INNER:].reshape(1, BC_DIM),
            pad_heads(dt_bias[l]), pad_heads(a_log[l]),
            jnp.repeat(d_skip[l].astype(F32), SSM_HEAD_DIM).reshape(1, D_INNER),
            ssm_norm_w[l].reshape(1, D_INNER), batch, seq)

        y_attn = _attention(
            main, lambda_q1[l].reshape(1, -1), lambda_k1[l].reshape(1, -1),
            lambda_q2[l].reshape(1, -1), lambda_k2[l].reshape(1, -1),
            subln_w[l].reshape(ATTN_V_DIM, 1), batch, seq, lambda_init)

        x2 = _merge(x2, y_ssm, y_attn, main, gate_b[l].reshape(1, 2 * D_MODEL),
                    w_proj_ssm[l].astype(BF16), w_proj_attn[l].astype(BF16),
                    w_out[l].astype(BF16), final_norm_w.reshape(1, D_MODEL),
                    final_norm=(l == depth - 1))
    return x2.reshape(batch, seq, D_MODEL)
```

```python
import functools
import math

import jax
import jax.numpy as jnp
from jax import lax
from jax.experimental import pallas as pl
from jax.experimental.pallas import tpu as pltpu

F32 = jnp.float32
BF16 = jnp.bfloat16

D_MODEL = 1024
EPS = 1e-5
D_INNER = 2 * D_MODEL
SSM_HEAD_DIM = 64
SSM_HEADS = D_INNER // SSM_HEAD_DIM
SSM_GROUPS = 4
HEADS_PER_GROUP = SSM_HEADS // SSM_GROUPS
D_STATE = 128
CONV_WIDTH = 4
CHUNK = 128
BC_DIM = 2 * SSM_GROUPS * D_STATE
XBC_DIM = D_INNER + BC_DIM
ATTN_HEADS = 8
ATTN_HEAD_DIM = 64
ATTN_V_DIM = 2 * ATTN_HEAD_DIM
ATTN_WIDTH = ATTN_HEADS * ATTN_V_DIM
ROPE_THETA = 10000.0
LOG2_E = math.log2(math.e)
ATTN_BLOCK = 512

LANES = 128
CARRY_ROWS = 8

COL_TILE = 1024
N_COL_TILES = 11
MAIN_DIM = COL_TILE * N_COL_TILES
Q_TILE = 5
K_TILE = 6
DT_PAD = LANES

VMEM_LIMIT = 48 * 1024 * 1024


def _silu(x):
    return x * (1.0 / (1.0 + jnp.exp(-x)))


def _dot(a, b):
    return jnp.dot(a, b, preferred_element_type=F32)


def _dot_nt(a, b):
    return lax.dot_general(a, b, (((1,), (1,)), ((), ())), preferred_element_type=F32)


def _inproj_kernel(x_ref, nw_ref, w_ref, wdt_hi_ref, wdt_lo_ref, out_ref, dt_ref, h_ref):
    @pl.when(pl.program_id(1) == 0)
    def _():
        x = x_ref[...]
        var = jnp.mean(x * x, axis=-1, keepdims=True)
        h = x * lax.rsqrt(var + EPS) * nw_ref[...]
        h_hi = h.astype(BF16)
        h_ref[...] = h_hi
        h_lo = (h - h_hi.astype(F32)).astype(BF16)
        dt_ref[...] = (_dot(h_hi, wdt_hi_ref[...]) + _dot(h_lo, wdt_hi_ref[...])
                       + _dot(h_hi, wdt_lo_ref[...]))

    out_ref[...] = _dot(h_ref[...], w_ref[...]).astype(out_ref.dtype)


def _inproj(x2, norm_w, w_main, wdt_hi, wdt_lo, seq):
    tokens = x2.shape[0]
    tm = min(1024, seq)
    return pl.pallas_call(
        _inproj_kernel,
        grid=(tokens // tm, N_COL_TILES),
        in_specs=[
            pl.BlockSpec((tm, D_MODEL), lambda i, j: (i, 0)),
            pl.BlockSpec((1, D_MODEL), lambda i, j: (0, 0)),
            pl.BlockSpec((D_MODEL, COL_TILE), lambda i, j: (0, j)),
            pl.BlockSpec((D_MODEL, DT_PAD), lambda i, j: (0, 0)),
            pl.BlockSpec((D_MODEL, DT_PAD), lambda i, j: (0, 0)),
        ],
        out_specs=[
            pl.BlockSpec((tm, COL_TILE), lambda i, j: (i, j)),
            pl.BlockSpec((tm, DT_PAD), lambda i, j: (i, 0)),
        ],
        out_shape=[
            jax.ShapeDtypeStruct((tokens, MAIN_DIM), BF16),
            jax.ShapeDtypeStruct((tokens, DT_PAD), F32),
        ],
        scratch_shapes=[pltpu.VMEM((tm, D_MODEL), BF16)],
        compiler_params=pltpu.CompilerParams(
            dimension_semantics=("parallel", "arbitrary"), vmem_limit_bytes=VMEM_LIMIT),
        name="inproj",
    )(x2, norm_w, w_main, wdt_hi, wdt_lo)


def _conv_silu(u, carry_ref, w_ref, b_ref):
    ext = jnp.concatenate([carry_ref[...], u], axis=0)
    out = b_ref[...] + w_ref[CONV_WIDTH - 1:CONV_WIDTH, :] * u
    for k in range(1, CONV_WIDTH):
        shifted = ext[CARRY_ROWS - k:CARRY_ROWS - k + CHUNK, :]
        out = out + w_ref[CONV_WIDTH - 1 - k:CONV_WIDTH - k, :] * shifted
    carry_ref[...] = u[CHUNK - CARRY_ROWS:, :]
    return _silu(out)


def _split3(v):
    hi = v.astype(BF16)
    r = v - hi.astype(F32)
    mid = r.astype(BF16)
    lo = (r - mid.astype(F32)).astype(BF16)
    return hi, mid, lo


def _ssd_kernel(xs_ref, z_ref, bc_ref, dt_ref, cwx_ref, cbx_ref, cwbc_ref, cbbc_ref,
                dtb_ref, alog_ref, dskip_ref, nw_ref, y_ref,
                state_ref, carry_x_ref, carry_bc_ref, ybuf_ref):
    @pl.when(pl.program_id(1) == 0)
    def _():
        state_ref[...] = jnp.zeros_like(state_ref)
        carry_x_ref[...] = jnp.zeros_like(carry_x_ref)
        carry_bc_ref[...] = jnp.zeros_like(carry_bc_ref)

    xs = _conv_silu(xs_ref[...].astype(F32), carry_x_ref, cwx_ref, cbx_ref)
    bc = _conv_silu(bc_ref[...].astype(F32), carry_bc_ref, cwbc_ref, cbbc_ref)

    dt_in = dt_ref[...] + dtb_ref[...]
    dt = jnp.maximum(dt_in, 0.0) + jnp.log1p(jnp.exp(-jnp.abs(dt_in)))
    adt = dt * (-jnp.exp(alog_ref[...]))

    row = lax.broadcasted_iota(jnp.int32, (CHUNK, CHUNK), 0)
    col = lax.broadcasted_iota(jnp.int32, (CHUNK, CHUNK), 1)
    tril = row >= col
    tril_bf = tril.astype(BF16)
    hi, mid, lo = _split3(adt)
    a_cs = _dot(tril_bf, hi) + _dot(tril_bf, mid) + _dot(tril_bf, lo)
    a_cs_t = a_cs.T
    dt_t = dt.T
    a_last_col = a_cs_t[:, CHUNK - 1:CHUNK]
    w_t = dt_t * jnp.exp(a_last_col - a_cs_t)
    chunk_decay = jnp.exp(a_cs[CHUNK - 1:CHUNK, :])

    for g in range(SSM_GROUPS):
        b_g = bc[:, g * D_STATE:(g + 1) * D_STATE]
        c_g = bc[:, BC_DIM // 2 + g * D_STATE:BC_DIM // 2 + (g + 1) * D_STATE]
        cb = _dot_nt(c_g.astype(BF16), b_g.astype(BF16))
        b_g_t = b_g.T
        for r in range(HEADS_PER_GROUP):
            h = g * HEADS_PER_GROUP + r
            col_b = jnp.broadcast_to(a_cs[:, h:h + 1], (CHUNK, CHUNK))
            seg = col_b - a_cs_t[h:h + 1, :]
            decay = jnp.exp(jnp.where(tril, seg, -jnp.inf))
            m = (cb * decay * dt_t[h:h + 1, :]).astype(BF16)
            x_h = xs[:, h * SSM_HEAD_DIM:(h + 1) * SSM_HEAD_DIM].astype(BF16)
            s_h = state_ref[h]
            c_scaled = (c_g * jnp.exp(col_b)).astype(BF16)
            y_h = _dot(m, x_h) + _dot(c_scaled, s_h.astype(BF16))
            ybuf_ref[:, h * SSM_HEAD_DIM:(h + 1) * SSM_HEAD_DIM] = y_h
            b_w = (b_g_t * w_t[h:h + 1, :]).astype(BF16)
            state_ref[h] = s_h * chunk_decay[:, h:h + 1] + _dot(b_w, x_h)

    y = ybuf_ref[...] + xs * dskip_ref[...]
    y = y * _silu(z_ref[...].astype(F32))
    gw = D_INNER // SSM_GROUPS
    for g in range(SSM_GROUPS):
        yg = y[:, g * gw:(g + 1) * gw]
        var = jnp.mean(yg * yg, axis=-1, keepdims=True)
        y_ref[:, g * gw:(g + 1) * gw] = (
            yg * lax.rsqrt(var + EPS) * nw_ref[:, g * gw:(g + 1) * gw]).astype(y_ref.dtype)


def _ssd(main, dt_raw, cwx, cbx, cwbc, cbbc, dtb, alog, dskip, nw, batch, seq):
    nc = seq // CHUNK
    tokens = batch * seq
    row_idx = lambda b, c: b * nc + c
    const = lambda b, c: (0, 0)
    return pl.pallas_call(
        _ssd_kernel,
        grid=(batch, nc),
        in_specs=[
            pl.BlockSpec((CHUNK, D_INNER), lambda b, c: (row_idx(b, c), 0)),
            pl.BlockSpec((CHUNK, D_INNER), lambda b, c: (row_idx(b, c), 1)),
            pl.BlockSpec((CHUNK, BC_DIM), lambda b, c: (row_idx(b, c), 4)),
            pl.BlockSpec((CHUNK, DT_PAD), lambda b, c: (row_idx(b, c), 0)),
            pl.BlockSpec((CONV_WIDTH, D_INNER), const),
            pl.BlockSpec((1, D_INNER), const),
            pl.BlockSpec((CONV_WIDTH, BC_DIM), const),
            pl.BlockSpec((1, BC_DIM), const),
            pl.BlockSpec((1, DT_PAD), const),
            pl.BlockSpec((1, DT_PAD), const),
            pl.BlockSpec((1, D_INNER), const),
            pl.BlockSpec((1, D_INNER), const),
        ],
        out_specs=pl.BlockSpec((CHUNK, D_INNER), lambda b, c: (row_idx(b, c), 0)),
        out_shape=jax.ShapeDtypeStruct((tokens, D_INNER), BF16),
        scratch_shapes=[
            pltpu.VMEM((SSM_HEADS, D_STATE, SSM_HEAD_DIM), F32),
            pltpu.VMEM((CARRY_ROWS, D_INNER), F32),
            pltpu.VMEM((CARRY_ROWS, BC_DIM), F32),
            pltpu.VMEM((CHUNK, D_INNER), F32),
        ],
        compiler_params=pltpu.CompilerParams(
            dimension_semantics=("parallel", "arbitrary"), vmem_limit_bytes=VMEM_LIMIT),
        name="ssd",
    )(main, main, main, dt_raw, cwx, cbx, cwbc, cbbc, dtb, alog, dskip, nw)


def _rotate_half_rows(x_t):
    half = ATTN_HEAD_DIM // 2
    parts = []
    for m in range(2):
        base = m * ATTN_HEAD_DIM
        parts += [x_t[base + half:base + 2 * half], x_t[base:base + half]]
    return jnp.concatenate(parts, axis=0)


def _rotate_half_lanes(x):
    half = ATTN_HEAD_DIM // 2
    lane = lax.broadcasted_iota(jnp.int32, x.shape, 1)
    return jnp.where((lane % ATTN_HEAD_DIM) < half,
                     pltpu.roll(x, LANES - half, 1), pltpu.roll(x, half, 1))


def _attn_kernel(q_ref, k_ref, v_ref, z_ref, cosq_ref, sinq_ref, cosk_ref, sink_ref,
                 lq1_ref, lk1_ref, lq2_ref, lk2_ref, sw_ref,
                 o_ref, vt_ref, kr_ref, qq_ref, s0_ref, s1_ref, mb0_ref, mb1_ref, p0_ref, p1_ref,
                 a0_ref, a1_ref, m_ref, l_ref, acc_ref, *, tq, tk, lambda_init):
    qi = pl.program_id(2)
    nkv = vt_ref.shape[0]

    @pl.when(qi == 0)
    def _():
        for c in range(nkv):
            rows = slice(c * tk, (c + 1) * tk)
            vt_ref[c] = v_ref[rows, :].astype(F32).T.astype(BF16)
            k = k_ref[rows, :].astype(F32)
            kr_ref[rows, :] = (k * cosk_ref[rows, :]
                               + _rotate_half_lanes(k) * sink_ref[rows, :]).astype(BF16)

    q_t = q_ref[...].astype(F32).T
    q_t = q_t * cosq_ref[...] + _rotate_half_rows(q_t) * sinq_ref[...]
    row = lax.broadcasted_iota(jnp.int32, q_t.shape, 0)
    qq_ref[...] = jnp.concatenate([jnp.where(row < ATTN_HEAD_DIM, q_t, 0.0),
                                   jnp.where(row >= ATTN_HEAD_DIM, q_t, 0.0)], axis=1).astype(BF16)

    m_ref[...] = jnp.full_like(m_ref, -jnp.inf)
    l_ref[...] = jnp.zeros_like(l_ref)
    acc_ref[...] = jnp.zeros_like(acc_ref)
    p1_ref[...] = jnp.zeros_like(p1_ref)
    a1_ref[...] = jnp.ones_like(a1_ref)

    def scores(j, s_ref, mb_ref):
        start = j * tk if isinstance(j, int) else pl.multiple_of(j * tk, tk)
        s = _dot(kr_ref[pl.ds(start, tk), :], qq_ref[...])
        s_ref[...] = s
        mb_ref[...] = jnp.max(s, axis=0, keepdims=True)

    def softmax(s_ref, mb_ref, p_ref, a_ref, diag_half=None):
        s = s_ref[...]
        if diag_half is None:
            mb = mb_ref[...]
        else:
            key = lax.broadcasted_iota(jnp.int32, s.shape, 0) + diag_half * tk
            qry = lax.broadcasted_iota(jnp.int32, s.shape, 1) & (tq - 1)
            s = jnp.where(key <= qry, s, -jnp.inf)
            mb = jnp.max(s, axis=0, keepdims=True)
        m_prev = m_ref[...]
        m_new = jnp.maximum(m_prev, mb)
        alpha = jnp.exp2(m_prev - m_new)
        p = jnp.exp2(s - m_new)
        l_ref[...] = alpha * l_ref[...] + jnp.sum(p, axis=0, keepdims=True)
        p_ref[...] = p.astype(BF16)
        a_ref[...] = alpha
        m_ref[...] = m_new

    def value(j, p_ref, a_ref):
        acc_ref[...] = acc_ref[...] * a_ref[...] + _dot(vt_ref[jnp.maximum(j, 0)], p_ref[...])

    scores(0, s0_ref, mb0_ref)

    def body(k, carry):
        softmax(s0_ref, mb0_ref, p0_ref, a0_ref)
        value(2 * k - 1, p1_ref, a1_ref)
        scores(2 * k + 1, s1_ref, mb1_ref)
        softmax(s1_ref, mb1_ref, p1_ref, a1_ref)
        value(2 * k, p0_ref, a0_ref)
        scores(2 * k + 2, s0_ref, mb0_ref)
        return carry

    lax.fori_loop(0, qi, body, 0)
    scores(2 * qi + 1, s1_ref, mb1_ref)
    softmax(s0_ref, mb0_ref, p0_ref, a0_ref, diag_half=0)
    value(2 * qi - 1, p1_ref, a1_ref)
    softmax(s1_ref, mb1_ref, p1_ref, a1_ref, diag_half=1)
    value(2 * qi, p0_ref, a0_ref)
    value(2 * qi + 1, p1_ref, a1_ref)

    lam = (jnp.exp(jnp.sum(lq1_ref[...] * lk1_ref[...], axis=-1, keepdims=True))
           - jnp.exp(jnp.sum(lq2_ref[...] * lk2_ref[...], axis=-1, keepdims=True))
           + lambda_init)
    o_maps = acc_ref[...] * (1.0 / l_ref[...])
    o = o_maps[:, :tq] - lam * o_maps[:, tq:]
    var = jnp.mean(o * o, axis=0, keepdims=True)
    o = o * lax.rsqrt(var + EPS) * sw_ref[...] * (1.0 - lambda_init)
    o_ref[...] = (o.T * _silu(z_ref[...].astype(F32))).astype(o_ref.dtype)


def _attention(main, rope, lq1, lk1, lq2, lk2, subln_w, batch, seq, lambda_init):
    tq = min(ATTN_BLOCK, seq)
    tk = tq // 2
    assert tq & (tq - 1) == 0 and seq % tq == 0
    nq = seq // tq
    tokens = batch * seq
    q_col = Q_TILE * COL_TILE // ATTN_V_DIM
    k_col = K_TILE * COL_TILE // ATTN_V_DIM
    v_col = (K_TILE + 1) * COL_TILE // ATTN_V_DIM
    z_col = (K_TILE + 2) * COL_TILE // ATTN_V_DIM
    const = lambda b, h, i: (0, 0)
    return pl.pallas_call(
        functools.partial(_attn_kernel, tq=tq, tk=tk, lambda_init=lambda_init),
        grid=(batch, ATTN_HEADS, nq),
        in_specs=[
            pl.BlockSpec((tq, ATTN_V_DIM), lambda b, h, i: (b * nq + i, q_col + h)),
            pl.BlockSpec((seq, ATTN_V_DIM), lambda b, h, i: (b, k_col + h)),
            pl.BlockSpec((seq, ATTN_V_DIM), lambda b, h, i: (b, v_col + h)),
            pl.BlockSpec((tq, ATTN_V_DIM), lambda b, h, i: (b * nq + i, z_col + h)),
            pl.BlockSpec((ATTN_V_DIM, tq), lambda b, h, i: (0, i)),
            pl.BlockSpec((ATTN_V_DIM, tq), lambda b, h, i: (0, i)),
            pl.BlockSpec((seq, ATTN_V_DIM), const),
            pl.BlockSpec((seq, ATTN_V_DIM), const),
            pl.BlockSpec((1, ATTN_HEAD_DIM), const),
            pl.BlockSpec((1, ATTN_HEAD_DIM), const),
            pl.BlockSpec((1, ATTN_HEAD_DIM), const),
            pl.BlockSpec((1, ATTN_HEAD_DIM), const),
            pl.BlockSpec((ATTN_V_DIM, 1), const),
        ],
        out_specs=pl.BlockSpec((tq, ATTN_V_DIM), lambda b, h, i: (b * nq + i, h)),
        out_shape=jax.ShapeDtypeStruct((tokens, ATTN_WIDTH), BF16),
        scratch_shapes=[
            pltpu.VMEM((seq // tk, ATTN_V_DIM, tk), BF16),
            pltpu.VMEM((seq, ATTN_V_DIM), BF16),
            pltpu.VMEM((ATTN_V_DIM, 2 * tq), BF16),
            pltpu.VMEM((tk, 2 * tq), F32),
            pltpu.VMEM((tk, 2 * tq), F32),
            pltpu.VMEM((1, 2 * tq), F32),
            pltpu.VMEM((1, 2 * tq), F32),
            pltpu.VMEM((tk, 2 * tq), BF16),
            pltpu.VMEM((tk, 2 * tq), BF16),
            pltpu.VMEM((1, 2 * tq), F32),
            pltpu.VMEM((1, 2 * tq), F32),
            pltpu.VMEM((1, 2 * tq), F32),
            pltpu.VMEM((1, 2 * tq), F32),
            pltpu.VMEM((ATTN_V_DIM, 2 * tq), F32),
        ],
        compiler_params=pltpu.CompilerParams(
            dimension_semantics=("parallel", "parallel", "arbitrary"),
            vmem_limit_bytes=VMEM_LIMIT),
        name="diff_attn",
    )(main, main, main, main, *rope, lq1, lk1, lq2, lk2, subln_w)


def _merge_kernel(x_ref, ys_ref, ya_ref, gs_ref, ga_ref, gb_ref, ps_ref, pa_ref, wo_ref, fw_ref,
                  o_ref, *, final_norm):
    proj_s = _dot(ys_ref[...], ps_ref[...])
    proj_a = _dot(ya_ref[...], pa_ref[...])
    g_s = 1.0 / (1.0 + jnp.exp(-(gs_ref[...].astype(F32) + gb_ref[:, :D_MODEL])))
    g_a = 1.0 / (1.0 + jnp.exp(-(ga_ref[...].astype(F32) + gb_ref[:, D_MODEL:])))
    merged = g_s * proj_s + g_a * proj_a
    x = x_ref[...] + _dot(merged.astype(BF16), wo_ref[...])
    if final_norm:
        var = jnp.mean(x * x, axis=-1, keepdims=True)
        x = x * lax.rsqrt(var + EPS) * fw_ref[...]
    o_ref[...] = x


def _merge(x2, y_ssm, y_attn, main, gate_b, p_ssm, p_attn, w_out, final_w, final_norm):
    tokens = x2.shape[0]
    tm = min(512, tokens)
    const = lambda i: (0, 0)
    return pl.pallas_call(
        functools.partial(_merge_kernel, final_norm=final_norm),
        grid=(tokens // tm,),
        in_specs=[
            pl.BlockSpec((tm, D_MODEL), lambda i: (i, 0)),
            pl.BlockSpec((tm, D_INNER), lambda i: (i, 0)),
            pl.BlockSpec((tm, ATTN_WIDTH), lambda i: (i, 0)),
            pl.BlockSpec((tm, D_MODEL), lambda i: (i, N_COL_TILES - 2)),
            pl.BlockSpec((tm, D_MODEL), lambda i: (i, N_COL_TILES - 1)),
            pl.BlockSpec((1, 2 * D_MODEL), const),
            pl.BlockSpec((D_INNER, D_MODEL), const),
            pl.BlockSpec((ATTN_WIDTH, D_MODEL), const),
            pl.BlockSpec((D_MODEL, D_MODEL), const),
            pl.BlockSpec((1, D_MODEL), const),
        ],
        out_specs=pl.BlockSpec((tm, D_MODEL), lambda i: (i, 0)),
        out_shape=jax.ShapeDtypeStruct((tokens, D_MODEL), F32),
        compiler_params=pltpu.CompilerParams(
            dimension_semantics=("parallel",), vmem_limit_bytes=VMEM_LIMIT),
        name="merge",
    )(x2, y_ssm, y_attn, main, main, gate_b, p_ssm, p_attn, w_out, final_w)


def _rope_tables(seq):
    inv = 1.0 / (ROPE_THETA ** (jnp.arange(0, ATTN_HEAD_DIM, 2, dtype=F32) / ATTN_HEAD_DIM))
    f = jnp.arange(seq, dtype=F32)[:, None] * inv[None, :]
    cos = jnp.cos(f)
    sin = jnp.sin(f)
    cos_k = jnp.concatenate([cos, cos, cos, cos], axis=-1)
    sin_k = jnp.concatenate([-sin, sin, -sin, sin], axis=-1)
    q_scale = ATTN_HEAD_DIM ** -0.5 * LOG2_E
    return (cos_k.T * q_scale, sin_k.T * q_scale, cos_k, sin_k)


def kernel(x, norm_w, w_in, conv_w, conv_b, dt_bias, a_log, d_skip, ssm_norm_w,
           lambda_q1, lambda_k1, lambda_q2, lambda_k2, subln_w, gate_b,
           w_proj_ssm, w_proj_attn, w_out, final_norm_w):
    batch, seq, _ = x.shape
    depth = norm_w.shape[0]
    assert seq % CHUNK == 0
    rope = _rope_tables(seq)
    x2 = x.reshape(batch * seq, D_MODEL)
    pad_heads = lambda v: jnp.pad(v.astype(F32), (0, DT_PAD - SSM_HEADS)).reshape(1, DT_PAD)

    dt_off = D_INNER + XBC_DIM
    for l in range(depth):
        lambda_init = 0.8 - 0.6 * math.exp(-0.3 * l)
        w = w_in[l]
        w_main = jnp.concatenate(
            [w[:, D_INNER:2 * D_INNER], w[:, :D_INNER], w[:, 2 * D_INNER:dt_off],
             w[:, dt_off + SSM_HEADS:]], axis=1).astype(BF16)
        w_dt = jnp.pad(w[:, dt_off:dt_off + SSM_HEADS], ((0, 0), (0, DT_PAD - SSM_HEADS)))
        wdt_hi = w_dt.astype(BF16)
        wdt_lo = (w_dt - wdt_hi.astype(F32)).astype(BF16)

        main, dt_raw = _inproj(x2, norm_w[l].reshape(1, D_MODEL), w_main, wdt_hi, wdt_lo, seq)

        y_ssm = _ssd(
            main, dt_raw,
            conv_w[l][:, :D_INNER], conv_b[l][:D_INNER].reshape(1, D_INNER),
            conv_w[l][:, D_INNER:], conv_b[l][D_INNER:].reshape(1, BC_DIM),
            pad_heads(dt_bias[l]), pad_heads(a_log[l]),
            jnp.repeat(d_skip[l].astype(F32), SSM_HEAD_DIM).reshape(1, D_INNER),
            ssm_norm_w[l].reshape(1, D_INNER), batch, seq)

        y_attn = _attention(
            main, rope, lambda_q1[l].reshape(1, -1), lambda_k1[l].reshape(1, -1),
            lambda_q2[l].reshape(1, -1), lambda_k2[l].reshape(1, -1),
            subln_w[l].reshape(ATTN_V_DIM, 1), batch, seq, lambda_init)

        x2 = _merge(x2, y_ssm, y_attn, main, gate_b[l].reshape(1, 2 * D_MODEL),
                    w_proj_ssm[l].astype(BF16), w_proj_attn[l].astype(BF16),
                    w_out[l].astype(BF16), final_norm_w.reshape(1, D_MODEL),
                    final_norm=(l == depth - 1))
    return x2.reshape(batch, seq, D_MODEL)
```

```python
import functools
import math

import jax
import jax.numpy as jnp
from jax import lax
from jax.experimental import pallas as pl
from jax.experimental.pallas import tpu as pltpu

F32 = jnp.float32
BF16 = jnp.bfloat16

D_MODEL = 1024
EPS = 1e-5
D_INNER = 2 * D_MODEL
SSM_HEAD_DIM = 64
SSM_HEADS = D_INNER // SSM_HEAD_DIM
SSM_GROUPS = 4
HEADS_PER_GROUP = SSM_HEADS // SSM_GROUPS
D_STATE = 128
CONV_WIDTH = 4
CHUNK = 128
BC_DIM = 2 * SSM_GROUPS * D_STATE
XBC_DIM = D_INNER + BC_DIM
ATTN_HEADS = 8
ATTN_HEAD_DIM = 64
ATTN_V_DIM = 2 * ATTN_HEAD_DIM
ATTN_WIDTH = ATTN_HEADS * ATTN_V_DIM
ROPE_THETA = 10000.0
LOG2_E = math.log2(math.e)
ATTN_BLOCK = 512

LANES = 128
PREV_ROWS = 16
CONV_SHIFT_ROWS = (CONV_WIDTH - 1) * CHUNK

COL_TILE = 1024
N_COL_TILES = 11
MAIN_DIM = COL_TILE * N_COL_TILES
Q_TILE = 5
K_TILE = 6
DT_PAD = LANES

VMEM_LIMIT = 48 * 1024 * 1024


def _silu(x):
    return x * (1.0 / (1.0 + jnp.exp2(x * -LOG2_E)))


def _dot(a, b):
    return jnp.dot(a, b, preferred_element_type=F32)


def _dot_nt(a, b):
    return lax.dot_general(a, b, (((1,), (1,)), ((), ())), preferred_element_type=F32)


def _inproj_kernel(x_ref, nw_ref, w_ref, wdt_hi_ref, wdt_lo_ref, out_ref, dt_ref, h_ref):
    @pl.when(pl.program_id(1) == 0)
    def _():
        x = x_ref[...]
        var = jnp.mean(x * x, axis=-1, keepdims=True)
        h = x * lax.rsqrt(var + EPS) * nw_ref[...]
        h_hi = h.astype(BF16)
        h_ref[...] = h_hi
        h_lo = (h - h_hi.astype(F32)).astype(BF16)
        dt_ref[...] = (_dot(h_hi, wdt_hi_ref[...]) + _dot(h_lo, wdt_hi_ref[...])
                       + _dot(h_hi, wdt_lo_ref[...]))

    out_ref[...] = _dot(h_ref[...], w_ref[...]).astype(out_ref.dtype)


def _inproj(x2, norm_w, w_main, wdt_hi, wdt_lo, seq):
    tokens = x2.shape[0]
    tm = min(1024, seq)
    return pl.pallas_call(
        _inproj_kernel,
        grid=(tokens // tm, N_COL_TILES),
        in_specs=[
            pl.BlockSpec((tm, D_MODEL), lambda i, j: (i, 0)),
            pl.BlockSpec((1, D_MODEL), lambda i, j: (0, 0)),
            pl.BlockSpec((D_MODEL, COL_TILE), lambda i, j: (0, j)),
            pl.BlockSpec((D_MODEL, DT_PAD), lambda i, j: (0, 0)),
            pl.BlockSpec((D_MODEL, DT_PAD), lambda i, j: (0, 0)),
        ],
        out_specs=[
            pl.BlockSpec((tm, COL_TILE), lambda i, j: (i, j)),
            pl.BlockSpec((tm, DT_PAD), lambda i, j: (i, 0)),
        ],
        out_shape=[
            jax.ShapeDtypeStruct((tokens, MAIN_DIM), BF16),
            jax.ShapeDtypeStruct((tokens, DT_PAD), F32),
        ],
        scratch_shapes=[pltpu.VMEM((tm, D_MODEL), BF16)],
        compiler_params=pltpu.CompilerParams(
            dimension_semantics=("parallel", "arbitrary"), vmem_limit_bytes=VMEM_LIMIT),
        name="inproj",
    )(x2, norm_w, w_main, wdt_hi, wdt_lo)


def _split3(v):
    hi = v.astype(BF16)
    r = v - hi.astype(F32)
    mid = r.astype(BF16)
    lo = (r - mid.astype(F32)).astype(BF16)
    return hi, mid, lo


def _conv_silu(u_ref, prev_ref, shift, w_ref, b_ref):
    u = u_ref[...]
    shifted = _dot(shift, jnp.concatenate([prev_ref[...], u], axis=0))
    out = b_ref[...] + w_ref[CONV_WIDTH - 1:CONV_WIDTH, :] * u.astype(F32)
    for k in range(1, CONV_WIDTH):
        out = out + w_ref[CONV_WIDTH - 1 - k:CONV_WIDTH - k, :] * shifted[(k - 1) * CHUNK:k * CHUNK, :]
    return _silu(out)


def _pad_rows_t(v_t):
    pad = jnp.zeros((LANES - v_t.shape[0], v_t.shape[1]), F32)
    return jnp.concatenate([v_t, pad], axis=0).T


def _split3_t(v_t):
    hi, mid, lo = _split3(v_t)
    return _pad_rows_t(jnp.concatenate(
        [hi.astype(F32), mid.astype(F32), lo.astype(F32)], axis=0)).astype(BF16)


def _ssd_kernel(xs_ref, z_ref, bc_ref, xs_prev_ref, bc_prev_ref, dt_ref, cwx_ref, cbx_ref,
                cwbc_ref, cbbc_ref, dtb_ref, alog_ref, dskip_ref, nw_ref, shift_ref, expand_ref,
                y_ref, state_ref):
    first = pl.program_id(1) == 0

    @pl.when(first)
    def _():
        state_ref[...] = jnp.zeros_like(state_ref)

    shift = shift_ref[jnp.where(first, 1, 0)]
    xs = _conv_silu(xs_ref, xs_prev_ref, shift, cwx_ref, cbx_ref)
    bc = _conv_silu(bc_ref, bc_prev_ref, shift, cwbc_ref, cbbc_ref)

    dt_in = dt_ref[...].T[:SSM_HEADS] + dtb_ref[...]
    dt = jnp.maximum(dt_in, 0.0) + jnp.log1p(jnp.exp(-jnp.abs(dt_in)))
    adt = dt * (-jnp.exp(alog_ref[...]))

    row = lax.broadcasted_iota(jnp.int32, (CHUNK, CHUNK), 0)
    col = lax.broadcasted_iota(jnp.int32, (CHUNK, CHUNK), 1)
    tril = row >= col
    triu_bf = (row <= col).astype(BF16)
    hi, mid, lo = _split3(adt)
    a_cs_t = _dot(hi, triu_bf) + _dot(mid, triu_bf) + _dot(lo, triu_bf)
    key_t = a_cs_t - jnp.log(dt)
    a_cs = _pad_rows_t(a_cs_t)
    e_acs_t = jnp.exp(a_cs_t)
    w_end_t = dt * jnp.exp(a_cs_t[:, CHUNK - 1:CHUNK] - a_cs_t)
    expanded = _dot(jnp.concatenate([_split3_t(e_acs_t), _split3_t(w_end_t)], axis=0),
                    expand_ref[...])
    e_acs_full = expanded[:CHUNK]
    xs_bf = xs.astype(BF16)
    xw_bf = (xs * expanded[CHUNK:]).astype(BF16)

    gw = D_INNER // SSM_GROUPS
    pair_w = 2 * SSM_HEAD_DIM
    lane = lax.broadcasted_iota(jnp.int32, (CHUNK, pair_w), 1)
    y_parts = []
    for g in range(SSM_GROUPS):
        b_g = bc[:, g * D_STATE:(g + 1) * D_STATE]
        c_bf = bc[:, BC_DIM // 2 + g * D_STATE:BC_DIM // 2 + (g + 1) * D_STATE].astype(BF16)
        cb = _dot_nt(c_bf, b_g.astype(BF16))
        s_g = state_ref[g]
        cols = slice(g * gw, (g + 1) * gw)
        y_off = _dot(c_bf, s_g.astype(BF16)) * e_acs_full[:, cols]
        for pr in range(HEADS_PER_GROUP // 2):
            h0 = g * HEADS_PER_GROUP + 2 * pr
            x_pair = xs_bf[:, h0 * SSM_HEAD_DIM:(h0 + 2) * SSM_HEAD_DIM]
            ys = []
            for h in (h0, h0 + 1):
                seg = a_cs[:, h:h + 1] - key_t[h:h + 1, :]
                m = (cb * jnp.exp(jnp.where(tril, seg, -jnp.inf))).astype(BF16)
                ys.append(_dot(m, x_pair))
            y_parts.append(jnp.where(lane < SSM_HEAD_DIM, ys[0], ys[1])
                           + y_off[:, pr * pair_w:(pr + 1) * pair_w])
        state_ref[g] = (s_g * e_acs_full[CHUNK - 1:CHUNK, cols]
                        + _dot(b_g.T.astype(BF16), xw_bf[:, cols]))

    y = jnp.concatenate(y_parts, axis=1) + xs * dskip_ref[...]
    y = y * _silu(z_ref[...].astype(F32))
    for g in range(SSM_GROUPS):
        yg = y[:, g * gw:(g + 1) * gw]
        var = jnp.mean(yg * yg, axis=-1, keepdims=True)
        y_ref[:, g * gw:(g + 1) * gw] = (
            yg * lax.rsqrt(var + EPS) * nw_ref[:, g * gw:(g + 1) * gw]).astype(y_ref.dtype)


def _ssd_constants():
    t = jnp.arange(CONV_SHIFT_ROWS)[:, None]
    j = jnp.arange(CHUNK + PREV_ROWS)[None, :]
    shift = (j == PREV_ROWS + (t % CHUNK) - (t // CHUNK + 1))
    shift = jnp.stack([shift, shift & (j >= PREV_ROWS)]).astype(BF16)
    r = jnp.arange(LANES)[:, None]
    c = jnp.arange(D_INNER)[None, :]
    expand = ((r < 3 * SSM_HEADS) & (r % SSM_HEADS == c // SSM_HEAD_DIM)).astype(BF16)
    return shift, expand


def _ssd(main, dt_raw, cwx, cbx, cwbc, cbbc, dtb, alog, dskip, nw, batch, seq):
    nc = seq // CHUNK
    tokens = batch * seq
    shift, expand = _ssd_constants()
    row_idx = lambda b, c: b * nc + c
    prev_idx = lambda b, c: jnp.maximum((b * nc + c) * (CHUNK // PREV_ROWS) - 1, 0)
    const = lambda b, c: (0, 0)
    return pl.pallas_call(
        _ssd_kernel,
        grid=(batch, nc),
        in_specs=[
            pl.BlockSpec((CHUNK, D_INNER), lambda b, c: (row_idx(b, c), 0)),
            pl.BlockSpec((CHUNK, D_INNER), lambda b, c: (row_idx(b, c), 1)),
            pl.BlockSpec((CHUNK, BC_DIM), lambda b, c: (row_idx(b, c), 4)),
            pl.BlockSpec((PREV_ROWS, D_INNER), lambda b, c: (prev_idx(b, c), 0)),
            pl.BlockSpec((PREV_ROWS, BC_DIM), lambda b, c: (prev_idx(b, c), 4)),
            pl.BlockSpec((CHUNK, DT_PAD), lambda b, c: (row_idx(b, c), 0)),
            pl.BlockSpec((CONV_WIDTH, D_INNER), const),
            pl.BlockSpec((1, D_INNER), const),
            pl.BlockSpec((CONV_WIDTH, BC_DIM), const),
            pl.BlockSpec((1, BC_DIM), const),
            pl.BlockSpec((SSM_HEADS, 1), const),
            pl.BlockSpec((SSM_HEADS, 1), const),
            pl.BlockSpec((1, D_INNER), const),
            pl.BlockSpec((1, D_INNER), const),
            pl.BlockSpec((2, CONV_SHIFT_ROWS, CHUNK + PREV_ROWS), lambda b, c: (0, 0, 0)),
            pl.BlockSpec((LANES, D_INNER), const),
        ],
        out_specs=pl.BlockSpec((CHUNK, D_INNER), lambda b, c: (row_idx(b, c), 0)),
        out_shape=jax.ShapeDtypeStruct((tokens, D_INNER), BF16),
        scratch_shapes=[
            pltpu.VMEM((SSM_GROUPS, D_STATE, D_INNER // SSM_GROUPS), F32),
        ],
        compiler_params=pltpu.CompilerParams(
            dimension_semantics=("parallel", "arbitrary"), vmem_limit_bytes=VMEM_LIMIT),
        name="ssd",
    )(main, main, main, main, main, dt_raw, cwx, cbx, cwbc, cbbc, dtb, alog, dskip, nw, shift, expand)


def _rotate_half_rows(x_t):
    half = ATTN_HEAD_DIM // 2
    parts = []
    for m in range(2):
        base = m * ATTN_HEAD_DIM
        parts += [x_t[base + half:base + 2 * half], x_t[base:base + half]]
    return jnp.concatenate(parts, axis=0)


def _rotate_half_lanes(x):
    half = ATTN_HEAD_DIM // 2
    lane = lax.broadcasted_iota(jnp.int32, x.shape, 1)
    return jnp.where((lane % ATTN_HEAD_DIM) < half,
                     pltpu.roll(x, LANES - half, 1), pltpu.roll(x, half, 1))


def _attn_kernel(q_ref, k_ref, v_ref, z_ref, cosq_ref, sinq_ref, cosk_ref, sink_ref,
                 lq1_ref, lk1_ref, lq2_ref, lk2_ref, sw_ref,
                 o_ref, vt_ref, kr_ref, qq_ref, s0_ref, s1_ref, mb0_ref, mb1_ref, p0_ref, p1_ref,
                 a0_ref, a1_ref, m_ref, l_ref, acc_ref, *, tq, tk, lambda_init, order, mxu_prep):
    qi = pl.program_id(2)
    nkv = vt_ref.shape[0]
    if mxu_prep:
        r_i = lax.broadcasted_iota(jnp.int32, (LANES, LANES), 0)
        c_i = lax.broadcasted_iota(jnp.int32, (LANES, LANES), 1)
        eye = (r_i == c_i).astype(BF16)
        half = ATTN_HEAD_DIM // 2
        partner = (r_i == jnp.where((c_i % ATTN_HEAD_DIM) < half, c_i + half, c_i - half)).astype(BF16)

    @pl.when(qi == 0)
    def _():
        for c in range(nkv):
            rows = slice(c * tk, (c + 1) * tk)
            if mxu_prep:
                vt_ref[c] = _dot_nt(eye, v_ref[rows, :]).astype(BF16)
                k_rot = _dot(k_ref[rows, :], partner)
            else:
                vt_ref[c] = v_ref[rows, :].astype(F32).T.astype(BF16)
                k_rot = _rotate_half_lanes(k_ref[rows, :].astype(F32))
            k = k_ref[rows, :].astype(F32)
            kr_ref[rows, :] = (k * cosk_ref[rows, :] + k_rot * sink_ref[rows, :]).astype(BF16)

    if mxu_prep:
        q_t = _dot_nt(eye, q_ref[...])
    else:
        q_t = q_ref[...].astype(F32).T
    q_t = q_t * cosq_ref[...] + _rotate_half_rows(q_t) * sinq_ref[...]
    row = lax.broadcasted_iota(jnp.int32, q_t.shape, 0)
    qq_ref[...] = jnp.concatenate([jnp.where(row < ATTN_HEAD_DIM, q_t, 0.0),
                                   jnp.where(row >= ATTN_HEAD_DIM, q_t, 0.0)], axis=1).astype(BF16)

    m_ref[...] = jnp.full_like(m_ref, -jnp.inf)
    l_ref[...] = jnp.zeros_like(l_ref)
    acc_ref[...] = jnp.zeros_like(acc_ref)
    p1_ref[...] = jnp.zeros_like(p1_ref)
    a1_ref[...] = jnp.ones_like(a1_ref)

    def scores(j, s_ref, mb_ref):
        start = j * tk if isinstance(j, int) else pl.multiple_of(j * tk, tk)
        s = _dot(kr_ref[pl.ds(start, tk), :], qq_ref[...])
        s_ref[...] = s
        mb_ref[...] = jnp.max(s, axis=0, keepdims=True)

    def softmax(s_ref, mb_ref, p_ref, a_ref, diag_half=None):
        s = s_ref[...]
        if diag_half is None:
            mb = mb_ref[...]
        else:
            key = lax.broadcasted_iota(jnp.int32, s.shape, 0) + diag_half * tk
            qry = lax.broadcasted_iota(jnp.int32, s.shape, 1) & (tq - 1)
            s = jnp.where(key <= qry, s, -jnp.inf)
            mb = jnp.max(s, axis=0, keepdims=True)
        m_prev = m_ref[...]
        m_new = jnp.maximum(m_prev, mb)
        alpha = jnp.exp2(m_prev - m_new)
        p = jnp.exp2(s - m_new)
        l_ref[...] = alpha * l_ref[...] + jnp.sum(p, axis=0, keepdims=True)
        p_ref[...] = p.astype(BF16)
        a_ref[...] = alpha
        m_ref[...] = m_new

    def value(j, p_ref, a_ref):
        acc_ref[...] = acc_ref[...] * a_ref[...] + _dot(vt_ref[jnp.maximum(j, 0)], p_ref[...])

    scores(0, s0_ref, mb0_ref)

    def body(k, carry):
        first = {"S": lambda: scores(2 * k + 1, s1_ref, mb1_ref),
                 "X": lambda: softmax(s0_ref, mb0_ref, p0_ref, a0_ref),
                 "V": lambda: value(2 * k - 1, p1_ref, a1_ref)}
        second = {"S": lambda: scores(2 * k + 2, s0_ref, mb0_ref),
                  "X": lambda: softmax(s1_ref, mb1_ref, p1_ref, a1_ref),
                  "V": lambda: value(2 * k, p0_ref, a0_ref)}
        for stage in (first, second):
            for part in order:
                stage[part]()
        return carry

    lax.fori_loop(0, qi, body, 0)
    scores(2 * qi + 1, s1_ref, mb1_ref)
    softmax(s0_ref, mb0_ref, p0_ref, a0_ref, diag_half=0)
    value(2 * qi - 1, p1_ref, a1_ref)
    softmax(s1_ref, mb1_ref, p1_ref, a1_ref, diag_half=1)
    value(2 * qi, p0_ref, a0_ref)
    value(2 * qi + 1, p1_ref, a1_ref)

    lam = (jnp.exp(jnp.sum(lq1_ref[...] * lk1_ref[...], axis=-1, keepdims=True))
           - jnp.exp(jnp.sum(lq2_ref[...] * lk2_ref[...], axis=-1, keepdims=True))
           + lambda_init)
    o_maps = acc_ref[...] * (1.0 / l_ref[...])
    o = o_maps[:, :tq] - lam * o_maps[:, tq:]
    var = jnp.mean(o * o, axis=0, keepdims=True)
    o = o * lax.rsqrt(var + EPS) * sw_ref[...] * (1.0 - lambda_init)
    o_ref[...] = (o.T * _silu(z_ref[...].astype(F32))).astype(o_ref.dtype)


def _attention(main, rope, lq1, lk1, lq2, lk2, subln_w, batch, seq, lambda_init, order, mxu_prep, name):
    tq = min(ATTN_BLOCK, seq)
    tk = tq // 2
    assert tq & (tq - 1) == 0 and seq % tq == 0
    nq = seq // tq
    tokens = batch * seq
    q_col = Q_TILE * COL_TILE // ATTN_V_DIM
    k_col = K_TILE * COL_TILE // ATTN_V_DIM
    v_col = (K_TILE + 1) * COL_TILE // ATTN_V_DIM
    z_col = (K_TILE + 2) * COL_TILE // ATTN_V_DIM
    const = lambda b, h, i: (0, 0)
    return pl.pallas_call(
        functools.partial(_attn_kernel, tq=tq, tk=tk, lambda_init=lambda_init, order=order, mxu_prep=mxu_prep),
        grid=(batch, ATTN_HEADS, nq),
        in_specs=[
            pl.BlockSpec((tq, ATTN_V_DIM), lambda b, h, i: (b * nq + i, q_col + h)),
            pl.BlockSpec((seq, ATTN_V_DIM), lambda b, h, i: (b, k_col + h)),
            pl.BlockSpec((seq, ATTN_V_DIM), lambda b, h, i: (b, v_col + h)),
            pl.BlockSpec((tq, ATTN_V_DIM), lambda b, h, i: (b * nq + i, z_col + h)),
            pl.BlockSpec((ATTN_V_DIM, tq), lambda b, h, i: (0, i)),
            pl.BlockSpec((ATTN_V_DIM, tq), lambda b, h, i: (0, i)),
            pl.BlockSpec((seq, ATTN_V_DIM), const),
            pl.BlockSpec((seq, ATTN_V_DIM), const),
            pl.BlockSpec((1, ATTN_HEAD_DIM), const),
            pl.BlockSpec((1, ATTN_HEAD_DIM), const),
            pl.BlockSpec((1, ATTN_HEAD_DIM), const),
            pl.BlockSpec((1, ATTN_HEAD_DIM), const),
            pl.BlockSpec((ATTN_V_DIM, 1), const),
        ],
        out_specs=pl.BlockSpec((tq, ATTN_V_DIM), lambda b, h, i: (b * nq + i, h)),
        out_shape=jax.ShapeDtypeStruct((tokens, ATTN_WIDTH), BF16),
        scratch_shapes=[
            pltpu.VMEM((seq // tk, ATTN_V_DIM, tk), BF16),
            pltpu.VMEM((seq, ATTN_V_DIM), BF16),
            pltpu.VMEM((ATTN_V_DIM, 2 * tq), BF16),
            pltpu.VMEM((tk, 2 * tq), F32),
            pltpu.VMEM((tk, 2 * tq), F32),
            pltpu.VMEM((1, 2 * tq), F32),
            pltpu.VMEM((1, 2 * tq), F32),
            pltpu.VMEM((tk, 2 * tq), BF16),
            pltpu.VMEM((tk, 2 * tq), BF16),
            pltpu.VMEM((1, 2 * tq), F32),
            pltpu.VMEM((1, 2 * tq), F32),
            pltpu.VMEM((1, 2 * tq), F32),
            pltpu.VMEM((1, 2 * tq), F32),
            pltpu.VMEM((ATTN_V_DIM, 2 * tq), F32),
        ],
        compiler_params=pltpu.CompilerParams(
            dimension_semantics=("parallel", "parallel", "arbitrary"),
            vmem_limit_bytes=VMEM_LIMIT),
        name=name,
    )(main, main, main, main, *rope, lq1, lk1, lq2, lk2, subln_w)


def _merge_kernel(x_ref, ys_ref, ya_ref, gs_ref, ga_ref, gb_ref, ps_ref, pa_ref, wo_ref, fw_ref,
                  o_ref, *, final_norm):
    proj_s = _dot(ys_ref[...], ps_ref[...])
    proj_a = _dot(ya_ref[...], pa_ref[...])
    g_s = 1.0 / (1.0 + jnp.exp(-(gs_ref[...].astype(F32) + gb_ref[:, :D_MODEL])))
    g_a = 1.0 / (1.0 + jnp.exp(-(ga_ref[...].astype(F32) + gb_ref[:, D_MODEL:])))
    merged = g_s * proj_s + g_a * proj_a
    x = x_ref[...] + _dot(merged.astype(BF16), wo_ref[...])
    if final_norm:
        var = jnp.mean(x * x, axis=-1, keepdims=True)
        x = x * lax.rsqrt(var + EPS) * fw_ref[...]
    o_ref[...] = x


def _merge(x2, y_ssm, y_attn, main, gate_b, p_ssm, p_attn, w_out, final_w, final_norm):
    tokens = x2.shape[0]
    tm = min(512, tokens)
    const = lambda i: (0, 0)
    return pl.pallas_call(
        functools.partial(_merge_kernel, final_norm=final_norm),
        grid=(tokens // tm,),
        in_specs=[
            pl.BlockSpec((tm, D_MODEL), lambda i: (i, 0)),
            pl.BlockSpec((tm, D_INNER), lambda i: (i, 0)),
            pl.BlockSpec((tm, ATTN_WIDTH), lambda i: (i, 0)),
            pl.BlockSpec((tm, D_MODEL), lambda i: (i, N_COL_TILES - 2)),
            pl.BlockSpec((tm, D_MODEL), lambda i: (i, N_COL_TILES - 1)),
            pl.BlockSpec((1, 2 * D_MODEL), const),
            pl.BlockSpec((D_INNER, D_MODEL), const),
            pl.BlockSpec((ATTN_WIDTH, D_MODEL), const),
            pl.BlockSpec((D_MODEL, D_MODEL), const),
            pl.BlockSpec((1, D_MODEL), const),
        ],
        out_specs=pl.BlockSpec((tm, D_MODEL), lambda i: (i, 0)),
        out_shape=jax.ShapeDtypeStruct((tokens, D_MODEL), F32),
        compiler_params=pltpu.CompilerParams(
            dimension_semantics=("parallel",), vmem_limit_bytes=VMEM_LIMIT),
        name="merge",
    )(x2, y_ssm, y_attn, main, main, gate_b, p_ssm, p_attn, w_out, final_w)


def _rope_tables(seq):
    inv = 1.0 / (ROPE_THETA ** (jnp.arange(0, ATTN_HEAD_DIM, 2, dtype=F32) / ATTN_HEAD_DIM))
    f = jnp.arange(seq, dtype=F32)[:, None] * inv[None, :]
    cos = jnp.cos(f)
    sin = jnp.sin(f)
    cos_k = jnp.concatenate([cos, cos, cos, cos], axis=-1)
    sin_k = jnp.concatenate([-sin, sin, -sin, sin], axis=-1)
    q_scale = ATTN_HEAD_DIM ** -0.5 * LOG2_E
    return (cos_k.T * q_scale, sin_k.T * q_scale, cos_k, sin_k)


def kernel(x, norm_w, w_in, conv_w, conv_b, dt_bias, a_log, d_skip, ssm_norm_w,
           lambda_q1, lambda_k1, lambda_q2, lambda_k2, subln_w, gate_b,
           w_proj_ssm, w_proj_attn, w_out, final_norm_w):
    batch, seq, _ = x.shape
    depth = norm_w.shape[0]
    assert seq % CHUNK == 0
    rope = _rope_tables(seq)
    x2 = x.reshape(batch * seq, D_MODEL)
    head_col = lambda v: v.astype(F32).reshape(SSM_HEADS, 1)

    dt_off = D_INNER + XBC_DIM
    for l in range(depth):
        lambda_init = 0.8 - 0.6 * math.exp(-0.3 * l)
        w = w_in[l]
        w_main = jnp.concatenate(
            [w[:, D_INNER:2 * D_INNER], w[:, :D_INNER], w[:, 2 * D_INNER:dt_off],
             w[:, dt_off + SSM_HEADS:]], axis=1).astype(BF16)
        w_dt = jnp.pad(w[:, dt_off:dt_off + SSM_HEADS], ((0, 0), (0, DT_PAD - SSM_HEADS)))
        wdt_hi = w_dt.astype(BF16)
        wdt_lo = (w_dt - wdt_hi.astype(F32)).astype(BF16)

        main, dt_raw = _inproj(x2, norm_w[l].reshape(1, D_MODEL), w_main, wdt_hi, wdt_lo, seq)

        y_ssm = _ssd(
            main, dt_raw,
            conv_w[l][:, :D_INNER], conv_b[l][:D_INNER].reshape(1, D_INNER),
            conv_w[l][:, D_INNER:], conv_b[l][D_INNER:].reshape(1, BC_DIM),
            head_col(dt_bias[l]), head_col(a_log[l]),
            jnp.repeat(d_skip[l].astype(F32), SSM_HEAD_DIM).reshape(1, D_INNER),
            ssm_norm_w[l].reshape(1, D_INNER), batch, seq)

        attn_args = (main, rope, lambda_q1[l].reshape(1, -1), lambda_k1[l].reshape(1, -1),
                     lambda_q2[l].reshape(1, -1), lambda_k2[l].reshape(1, -1),
                     subln_w[l].reshape(ATTN_V_DIM, 1), batch, seq, lambda_init)
        y_a = _attention(*attn_args, "XVS", False, "diff_attn_xvs")
        y_b = _attention(*attn_args, "SVX", False, "diff_attn_svx")
        y_c = _attention(*attn_args, "XVS", True, "diff_attn_mxuprep")
        y_attn = jnp.concatenate([y_a[:, :384], y_b[:, 384:768], y_c[:, 768:]], axis=1)

        x2 = _merge(x2, y_ssm, y_attn, main, gate_b[l].reshape(1, 2 * D_MODEL),
                    w_proj_ssm[l].astype(BF16), w_proj_attn[l].astype(BF16),
                    w_out[l].astype(BF16), final_norm_w.reshape(1, D_MODEL),
                    final_norm=(l == depth - 1))
    return x2.reshape(batch, seq, D_MODEL)
```

```python
import functools
import math

import jax
import jax.numpy as jnp
from jax import lax
from jax.experimental import pallas as pl
from jax.experimental.pallas import tpu as pltpu

F32 = jnp.float32
BF16 = jnp.bfloat16

D_MODEL = 1024
EPS = 1e-5
D_INNER = 2 * D_MODEL
SSM_HEAD_DIM = 64
SSM_HEADS = D_INNER // SSM_HEAD_DIM
SSM_GROUPS = 4
HEADS_PER_GROUP = SSM_HEADS // SSM_GROUPS
D_STATE = 128
CONV_WIDTH = 4
CHUNK = 128
BC_DIM = 2 * SSM_GROUPS * D_STATE
XBC_DIM = D_INNER + BC_DIM
ATTN_HEADS = 8
ATTN_HEAD_DIM = 64
ATTN_V_DIM = 2 * ATTN_HEAD_DIM
ATTN_WIDTH = ATTN_HEADS * ATTN_V_DIM
ROPE_THETA = 10000.0
LOG2_E = math.log2(math.e)
ATTN_BLOCK = 512
VT_ROWS = ATTN_V_DIM + 16

LANES = 128
PREV_ROWS = 16
CONV_SHIFT_ROWS = (CONV_WIDTH - 1) * CHUNK

COL_TILE = 1024
N_COL_TILES = 11
MAIN_DIM = COL_TILE * N_COL_TILES
Q_TILE = 5
K_TILE = 6
DT_PAD = LANES

VMEM_LIMIT = 48 * 1024 * 1024


def _silu(x):
    return x * (1.0 / (1.0 + jnp.exp2(x * -LOG2_E)))


def _dot(a, b):
    return jnp.dot(a, b, preferred_element_type=F32)


def _dot_nt(a, b):
    return lax.dot_general(a, b, (((1,), (1,)), ((), ())), preferred_element_type=F32)


def _inproj_kernel(x_ref, nw_ref, w_ref, wdt_hi_ref, wdt_lo_ref, out_ref, dt_ref, h_ref):
    @pl.when(pl.program_id(1) == 0)
    def _():
        x = x_ref[...]
        var = jnp.mean(x * x, axis=-1, keepdims=True)
        h = x * lax.rsqrt(var + EPS) * nw_ref[...]
        h_hi = h.astype(BF16)
        h_ref[...] = h_hi
        h_lo = (h - h_hi.astype(F32)).astype(BF16)
        dt_ref[...] = (_dot(h_hi, wdt_hi_ref[...]) + _dot(h_lo, wdt_hi_ref[...])
                       + _dot(h_hi, wdt_lo_ref[...]))

    out_ref[...] = _dot(h_ref[...], w_ref[...]).astype(out_ref.dtype)


def _inproj(x2, norm_w, w_main, wdt_hi, wdt_lo, seq):
    tokens = x2.shape[0]
    tm = min(1024, seq)
    return pl.pallas_call(
        _inproj_kernel,
        grid=(tokens // tm, N_COL_TILES),
        in_specs=[
            pl.BlockSpec((tm, D_MODEL), lambda i, j: (i, 0)),
            pl.BlockSpec((1, D_MODEL), lambda i, j: (0, 0)),
            pl.BlockSpec((D_MODEL, COL_TILE), lambda i, j: (0, j)),
            pl.BlockSpec((D_MODEL, DT_PAD), lambda i, j: (0, 0)),
            pl.BlockSpec((D_MODEL, DT_PAD), lambda i, j: (0, 0)),
        ],
        out_specs=[
            pl.BlockSpec((tm, COL_TILE), lambda i, j: (i, j)),
            pl.BlockSpec((tm, DT_PAD), lambda i, j: (i, 0)),
        ],
        out_shape=[
            jax.ShapeDtypeStruct((tokens, MAIN_DIM), BF16),
            jax.ShapeDtypeStruct((tokens, DT_PAD), F32),
        ],
        scratch_shapes=[pltpu.VMEM((tm, D_MODEL), BF16)],
        compiler_params=pltpu.CompilerParams(
            dimension_semantics=("parallel", "arbitrary"), vmem_limit_bytes=VMEM_LIMIT),
        name="inproj",
    )(x2, norm_w, w_main, wdt_hi, wdt_lo)


def _split3(v):
    hi = v.astype(BF16)
    r = v - hi.astype(F32)
    mid = r.astype(BF16)
    lo = (r - mid.astype(F32)).astype(BF16)
    return hi, mid, lo


def _conv_silu(u_ref, prev_ref, shift, w_ref, b_ref):
    u = u_ref[...]
    shifted = _dot(shift, jnp.concatenate([prev_ref[...], u], axis=0))
    out = b_ref[...] + w_ref[CONV_WIDTH - 1:CONV_WIDTH, :] * u.astype(F32)
    for k in range(1, CONV_WIDTH):
        out = out + w_ref[CONV_WIDTH - 1 - k:CONV_WIDTH - k, :] * shifted[(k - 1) * CHUNK:k * CHUNK, :]
    return _silu(out)


def _pad_rows_t(v_t):
    pad = jnp.zeros((LANES - v_t.shape[0], v_t.shape[1]), F32)
    return jnp.concatenate([v_t, pad], axis=0).T


def _split3_t(v_t):
    hi, mid, lo = _split3(v_t)
    return _pad_rows_t(jnp.concatenate(
        [hi.astype(F32), mid.astype(F32), lo.astype(F32)], axis=0)).astype(BF16)


def _ssd_kernel(xs_ref, z_ref, bc_ref, xs_prev_ref, bc_prev_ref, dt_ref, cwx_ref, cbx_ref,
                cwbc_ref, cbbc_ref, dtb_ref, alog_ref, dskip_ref, nw_ref, shift_ref, expand_ref,
                y_ref, state_ref):
    first = pl.program_id(1) == 0

    @pl.when(first)
    def _():
        state_ref[...] = jnp.zeros_like(state_ref)

    shift = shift_ref[jnp.where(first, 1, 0)]
    xs = _conv_silu(xs_ref, xs_prev_ref, shift, cwx_ref, cbx_ref)
    bc = _conv_silu(bc_ref, bc_prev_ref, shift, cwbc_ref, cbbc_ref)

    dt_in = dt_ref[...].T[:SSM_HEADS] + dtb_ref[...]
    dt = jnp.maximum(dt_in, 0.0) + jnp.log1p(jnp.exp(-jnp.abs(dt_in)))
    adt = dt * (-jnp.exp(alog_ref[...]))

    row = lax.broadcasted_iota(jnp.int32, (CHUNK, CHUNK), 0)
    col = lax.broadcasted_iota(jnp.int32, (CHUNK, CHUNK), 1)
    tril = row >= col
    triu_bf = (row <= col).astype(BF16)
    hi, mid, lo = _split3(adt)
    a_cs_t = _dot(hi, triu_bf) + _dot(mid, triu_bf) + _dot(lo, triu_bf)
    key_t = a_cs_t - jnp.log(dt)
    a_cs = _pad_rows_t(a_cs_t)
    e_acs_t = jnp.exp(a_cs_t)
    w_end_t = dt * jnp.exp(a_cs_t[:, CHUNK - 1:CHUNK] - a_cs_t)
    expanded = _dot(jnp.concatenate([_split3_t(e_acs_t), _split3_t(w_end_t)], axis=0),
                    expand_ref[...])
    e_acs_full = expanded[:CHUNK]
    xs_bf = xs.astype(BF16)
    xw_bf = (xs * expanded[CHUNK:]).astype(BF16)

    gw = D_INNER // SSM_GROUPS
    pair_w = 2 * SSM_HEAD_DIM
    lane = lax.broadcasted_iota(jnp.int32, (CHUNK, pair_w), 1)
    y_parts = []
    for g in range(SSM_GROUPS):
        b_g = bc[:, g * D_STATE:(g + 1) * D_STATE]
        c_bf = bc[:, BC_DIM // 2 + g * D_STATE:BC_DIM // 2 + (g + 1) * D_STATE].astype(BF16)
        cb = _dot_nt(c_bf, b_g.astype(BF16))
        s_g = state_ref[g]
        cols = slice(g * gw, (g + 1) * gw)
        y_off = _dot(c_bf, s_g.astype(BF16)) * e_acs_full[:, cols]
        for pr in range(HEADS_PER_GROUP // 2):
            h0 = g * HEADS_PER_GROUP + 2 * pr
            x_pair = xs_bf[:, h0 * SSM_HEAD_DIM:(h0 + 2) * SSM_HEAD_DIM]
            ys = []
            for h in (h0, h0 + 1):
                seg = a_cs[:, h:h + 1] - key_t[h:h + 1, :]
                m = (cb * jnp.exp(jnp.where(tril, seg, -jnp.inf))).astype(BF16)
                ys.append(_dot(m, x_pair))
            y_parts.append(jnp.where(lane < SSM_HEAD_DIM, ys[0], ys[1])
                           + y_off[:, pr * pair_w:(pr + 1) * pair_w])
        state_ref[g] = (s_g * e_acs_full[CHUNK - 1:CHUNK, cols]
                        + _dot(b_g.T.astype(BF16), xw_bf[:, cols]))

    y = jnp.concatenate(y_parts, axis=1) + xs * dskip_ref[...]
    y = y * _silu(z_ref[...].astype(F32))
    for g in range(SSM_GROUPS):
        yg = y[:, g * gw:(g + 1) * gw]
        var = jnp.mean(yg * yg, axis=-1, keepdims=True)
        y_ref[:, g * gw:(g + 1) * gw] = (
            yg * lax.rsqrt(var + EPS) * nw_ref[:, g * gw:(g + 1) * gw]).astype(y_ref.dtype)


def _ssd_constants():
    t = jnp.arange(CONV_SHIFT_ROWS)[:, None]
    j = jnp.arange(CHUNK + PREV_ROWS)[None, :]
    shift = (j == PREV_ROWS + (t % CHUNK) - (t // CHUNK + 1))
    shift = jnp.stack([shift, shift & (j >= PREV_ROWS)]).astype(BF16)
    r = jnp.arange(LANES)[:, None]
    c = jnp.arange(D_INNER)[None, :]
    expand = ((r < 3 * SSM_HEADS) & (r % SSM_HEADS == c // SSM_HEAD_DIM)).astype(BF16)
    return shift, expand


def _ssd(main, dt_raw, cwx, cbx, cwbc, cbbc, dtb, alog, dskip, nw, batch, seq):
    nc = seq // CHUNK
    tokens = batch * seq
    shift, expand = _ssd_constants()
    row_idx = lambda b, c: b * nc + c
    prev_idx = lambda b, c: jnp.maximum((b * nc + c) * (CHUNK // PREV_ROWS) - 1, 0)
    const = lambda b, c: (0, 0)
    return pl.pallas_call(
        _ssd_kernel,
        grid=(batch, nc),
        in_specs=[
            pl.BlockSpec((CHUNK, D_INNER), lambda b, c: (row_idx(b, c), 0)),
            pl.BlockSpec((CHUNK, D_INNER), lambda b, c: (row_idx(b, c), 1)),
            pl.BlockSpec((CHUNK, BC_DIM), lambda b, c: (row_idx(b, c), 4)),
            pl.BlockSpec((PREV_ROWS, D_INNER), lambda b, c: (prev_idx(b, c), 0)),
            pl.BlockSpec((PREV_ROWS, BC_DIM), lambda b, c: (prev_idx(b, c), 4)),
            pl.BlockSpec((CHUNK, DT_PAD), lambda b, c: (row_idx(b, c), 0)),
            pl.BlockSpec((CONV_WIDTH, D_INNER), const),
            pl.BlockSpec((1, D_INNER), const),
            pl.BlockSpec((CONV_WIDTH, BC_DIM), const),
            pl.BlockSpec((1, BC_DIM), const),
            pl.BlockSpec((SSM_HEADS, 1), const),
            pl.BlockSpec((SSM_HEADS, 1), const),
            pl.BlockSpec((1, D_INNER), const),
            pl.BlockSpec((1, D_INNER), const),
            pl.BlockSpec((2, CONV_SHIFT_ROWS, CHUNK + PREV_ROWS), lambda b, c: (0, 0, 0)),
            pl.BlockSpec((LANES, D_INNER), const),
        ],
        out_specs=pl.BlockSpec((CHUNK, D_INNER), lambda b, c: (row_idx(b, c), 0)),
        out_shape=jax.ShapeDtypeStruct((tokens, D_INNER), BF16),
        scratch_shapes=[
            pltpu.VMEM((SSM_GROUPS, D_STATE, D_INNER // SSM_GROUPS), F32),
        ],
        compiler_params=pltpu.CompilerParams(
            dimension_semantics=("parallel", "arbitrary"), vmem_limit_bytes=VMEM_LIMIT),
        name="ssd",
    )(main, main, main, main, main, dt_raw, cwx, cbx, cwbc, cbbc, dtb, alog, dskip, nw, shift, expand)


def _rotate_half_rows(x_t):
    half = ATTN_HEAD_DIM // 2
    parts = []
    for m in range(2):
        base = m * ATTN_HEAD_DIM
        parts += [x_t[base + half:base + 2 * half], x_t[base:base + half]]
    return jnp.concatenate(parts, axis=0)


def _rotate_half_lanes(x):
    half = ATTN_HEAD_DIM // 2
    lane = lax.broadcasted_iota(jnp.int32, x.shape, 1)
    return jnp.where((lane % ATTN_HEAD_DIM) < half,
                     pltpu.roll(x, LANES - half, 1), pltpu.roll(x, half, 1))


def _attn_kernel(q_ref, k_ref, v_ref, z_ref, cosq_ref, sinq_ref, cosk_ref, sink_ref,
                 lq1_ref, lk1_ref, lq2_ref, lk2_ref, sw_ref,
                 o_ref, vt_ref, kr_ref, qq_ref, s0_ref, s1_ref, mb0_ref, mb1_ref, p0_ref, p1_ref,
                 a0_ref, a1_ref, m_ref, acc_ref, *, tq, tk, lambda_init):
    qi = pl.program_id(2)
    nkv = vt_ref.shape[0]

    @pl.when(qi == 0)
    def _():
        ones_rows = (lax.broadcasted_iota(jnp.int32, (VT_ROWS - ATTN_V_DIM, tk), 0) == 0).astype(BF16)
        for c in range(nkv):
            rows = slice(c * tk, (c + 1) * tk)
            vt_ref[c, :ATTN_V_DIM, :] = v_ref[rows, :].astype(F32).T.astype(BF16)
            vt_ref[c, ATTN_V_DIM:, :] = ones_rows
            k = k_ref[rows, :].astype(F32)
            kr_ref[rows, :] = (k * cosk_ref[rows, :]
                               + _rotate_half_lanes(k) * sink_ref[rows, :]).astype(BF16)

    q_t = q_ref[...].astype(F32).T
    q_t = q_t * cosq_ref[...] + _rotate_half_rows(q_t) * sinq_ref[...]
    row = lax.broadcasted_iota(jnp.int32, q_t.shape, 0)
    qq_ref[...] = jnp.concatenate([jnp.where(row < ATTN_HEAD_DIM, q_t, 0.0),
                                   jnp.where(row >= ATTN_HEAD_DIM, q_t, 0.0)], axis=1).astype(BF16)

    m_ref[...] = jnp.full_like(m_ref, -jnp.inf)
    acc_ref[...] = jnp.zeros_like(acc_ref)
    p1_ref[...] = jnp.zeros_like(p1_ref)
    a1_ref[...] = jnp.ones_like(a1_ref)

    def scores(j, s_ref, mb_ref):
        start = j * tk if isinstance(j, int) else pl.multiple_of(j * tk, tk)
        s = _dot(kr_ref[pl.ds(start, tk), :], qq_ref[...])
        s_ref[...] = s
        mb_ref[...] = jnp.max(s, axis=0, keepdims=True)

    def softmax(s_ref, mb_ref, p_ref, a_ref, diag_half=None):
        s = s_ref[...]
        if diag_half is None:
            mb = mb_ref[...]
        else:
            key = lax.broadcasted_iota(jnp.int32, s.shape, 0) + diag_half * tk
            qry = lax.broadcasted_iota(jnp.int32, s.shape, 1) & (tq - 1)
            s = jnp.where(key <= qry, s, -jnp.inf)
            mb = jnp.max(s, axis=0, keepdims=True)
        m_prev = m_ref[...]
        m_new = jnp.maximum(m_prev, mb)
        a_ref[...] = jnp.exp2(m_prev - m_new)
        p_ref[...] = jnp.exp2((s - m_new).astype(BF16))
        m_ref[...] = m_new

    def value(j, p_ref, a_ref):
        acc_ref[...] = acc_ref[...] * a_ref[...] + _dot(vt_ref[jnp.maximum(j, 0)], p_ref[...])

    scores(0, s0_ref, mb0_ref)

    def body(k, carry):
        softmax(s0_ref, mb0_ref, p0_ref, a0_ref)
        value(2 * k - 1, p1_ref, a1_ref)
        scores(2 * k + 1, s1_ref, mb1_ref)
        softmax(s1_ref, mb1_ref, p1_ref, a1_ref)
        value(2 * k, p0_ref, a0_ref)
        scores(2 * k + 2, s0_ref, mb0_ref)
        return carry

    lax.fori_loop(0, qi, body, 0)
    scores(2 * qi + 1, s1_ref, mb1_ref)
    softmax(s0_ref, mb0_ref, p0_ref, a0_ref, diag_half=0)
    value(2 * qi - 1, p1_ref, a1_ref)
    softmax(s1_ref, mb1_ref, p1_ref, a1_ref, diag_half=1)
    value(2 * qi, p0_ref, a0_ref)
    value(2 * qi + 1, p1_ref, a1_ref)

    lam = (jnp.exp(jnp.sum(lq1_ref[...] * lk1_ref[...], axis=-1, keepdims=True))
           - jnp.exp(jnp.sum(lq2_ref[...] * lk2_ref[...], axis=-1, keepdims=True))
           + lambda_init)
    o_maps = acc_ref[:ATTN_V_DIM, :] * (1.0 / acc_ref[ATTN_V_DIM:ATTN_V_DIM + 1, :])
    o = o_maps[:, :tq] - lam * o_maps[:, tq:]
    var = jnp.mean(o * o, axis=0, keepdims=True)
    o = o * lax.rsqrt(var + EPS) * sw_ref[...] * (1.0 - lambda_init)
    o_ref[...] = (o.T * _silu(z_ref[...].astype(F32))).astype(o_ref.dtype)


def _attention(main, rope, lq1, lk1, lq2, lk2, subln_w, batch, seq, lambda_init):
    tq = min(ATTN_BLOCK, seq)
    tk = tq // 2
    assert tq & (tq - 1) == 0 and seq % tq == 0
    nq = seq // tq
    tokens = batch * seq
    q_col = Q_TILE * COL_TILE // ATTN_V_DIM
    k_col = K_TILE * COL_TILE // ATTN_V_DIM
    v_col = (K_TILE + 1) * COL_TILE // ATTN_V_DIM
    z_col = (K_TILE + 2) * COL_TILE // ATTN_V_DIM
    const = lambda b, h, i: (0, 0)
    return pl.pallas_call(
        functools.partial(_attn_kernel, tq=tq, tk=tk, lambda_init=lambda_init),
        grid=(batch, ATTN_HEADS, nq),
        in_specs=[
            pl.BlockSpec((tq, ATTN_V_DIM), lambda b, h, i: (b * nq + i, q_col + h)),
            pl.BlockSpec((seq, ATTN_V_DIM), lambda b, h, i: (b, k_col + h)),
            pl.BlockSpec((seq, ATTN_V_DIM), lambda b, h, i: (b, v_col + h)),
            pl.BlockSpec((tq, ATTN_V_DIM), lambda b, h, i: (b * nq + i, z_col + h)),
            pl.BlockSpec((ATTN_V_DIM, tq), lambda b, h, i: (0, i)),
            pl.BlockSpec((ATTN_V_DIM, tq), lambda b, h, i: (0, i)),
            pl.BlockSpec((seq, ATTN_V_DIM), const),
            pl.BlockSpec((seq, ATTN_V_DIM), const),
            pl.BlockSpec((1, ATTN_HEAD_DIM), const),
            pl.BlockSpec((1, ATTN_HEAD_DIM), const),
            pl.BlockSpec((1, ATTN_HEAD_DIM), const),
            pl.BlockSpec((1, ATTN_HEAD_DIM), const),
            pl.BlockSpec((ATTN_V_DIM, 1), const),
        ],
        out_specs=pl.BlockSpec((tq, ATTN_V_DIM), lambda b, h, i: (b * nq + i, h)),
        out_shape=jax.ShapeDtypeStruct((tokens, ATTN_WIDTH), BF16),
        scratch_shapes=[
            pltpu.VMEM((seq // tk, VT_ROWS, tk), BF16),
            pltpu.VMEM((seq, ATTN_V_DIM), BF16),
            pltpu.VMEM((ATTN_V_DIM, 2 * tq), BF16),
            pltpu.VMEM((tk, 2 * tq), F32),
            pltpu.VMEM((tk, 2 * tq), F32),
            pltpu.VMEM((1, 2 * tq), F32),
            pltpu.VMEM((1, 2 * tq), F32),
            pltpu.VMEM((tk, 2 * tq), BF16),
            pltpu.VMEM((tk, 2 * tq), BF16),
            pltpu.VMEM((1, 2 * tq), F32),
            pltpu.VMEM((1, 2 * tq), F32),
            pltpu.VMEM((1, 2 * tq), F32),
            pltpu.VMEM((VT_ROWS, 2 * tq), F32),
        ],
        compiler_params=pltpu.CompilerParams(
            dimension_semantics=("parallel", "parallel", "arbitrary"),
            vmem_limit_bytes=VMEM_LIMIT),
        name="diff_attn",
    )(main, main, main, main, *rope, lq1, lk1, lq2, lk2, subln_w)


def _merge_kernel(x_ref, ys_ref, ya_ref, gs_ref, ga_ref, gb_ref, ps_ref, pa_ref, wo_ref, fw_ref,
                  o_ref, *, final_norm):
    proj_s = _dot(ys_ref[...], ps_ref[...])
    proj_a = _dot(ya_ref[...], pa_ref[...])
    g_s = 1.0 / (1.0 + jnp.exp(-(gs_ref[...].astype(F32) + gb_ref[:, :D_MODEL])))
    g_a = 1.0 / (1.0 + jnp.exp(-(ga_ref[...].astype(F32) + gb_ref[:, D_MODEL:])))
    merged = g_s * proj_s + g_a * proj_a
    x = x_ref[...] + _dot(merged.astype(BF16), wo_ref[...])
    if final_norm:
        var = jnp.mean(x * x, axis=-1, keepdims=True)
        x = x * lax.rsqrt(var + EPS) * fw_ref[...]
    o_ref[...] = x


def _merge(x2, y_ssm, y_attn, main, gate_b, p_ssm, p_attn, w_out, final_w, final_norm):
    tokens = x2.shape[0]
    tm = min(512, tokens)
    const = lambda i: (0, 0)
    return pl.pallas_call(
        functools.partial(_merge_kernel, final_norm=final_norm),
        grid=(tokens // tm,),
        in_specs=[
            pl.BlockSpec((tm, D_MODEL), lambda i: (i, 0)),
            pl.BlockSpec((tm, D_INNER), lambda i: (i, 0)),
            pl.BlockSpec((tm, ATTN_WIDTH), lambda i: (i, 0)),
            pl.BlockSpec((tm, D_MODEL), lambda i: (i, N_COL_TILES - 2)),
            pl.BlockSpec((tm, D_MODEL), lambda i: (i, N_COL_TILES - 1)),
            pl.BlockSpec((1, 2 * D_MODEL), const),
            pl.BlockSpec((D_INNER, D_MODEL), const),
            pl.BlockSpec((ATTN_WIDTH, D_MODEL), const),
            pl.BlockSpec((D_MODEL, D_MODEL), const),
            pl.BlockSpec((1, D_MODEL), const),
        ],
        out_specs=pl.BlockSpec((tm, D_MODEL), lambda i: (i, 0)),
        out_shape=jax.ShapeDtypeStruct((tokens, D_MODEL), F32),
        compiler_params=pltpu.CompilerParams(
            dimension_semantics=("parallel",), vmem_limit_bytes=VMEM_LIMIT),
        name="merge",
    )(x2, y_ssm, y_attn, main, main, gate_b, p_ssm, p_attn, w_out, final_w)


def _rope_tables(seq):
    inv = 1.0 / (ROPE_THETA ** (jnp.arange(0, ATTN_HEAD_DIM, 2, dtype=F32) / ATTN_HEAD_DIM))
    f = jnp.arange(seq, dtype=F32)[:, None] * inv[None, :]
    cos = jnp.cos(f)
    sin = jnp.sin(f)
    cos_k = jnp.concatenate([cos, cos, cos, cos], axis=-1)
    sin_k = jnp.concatenate([-sin, sin, -sin, sin], axis=-1)
    q_scale = ATTN_HEAD_DIM ** -0.5 * LOG2_E
    return (cos_k.T * q_scale, sin_k.T * q_scale, cos_k, sin_k)


def kernel(x, norm_w, w_in, conv_w, conv_b, dt_bias, a_log, d_skip, ssm_norm_w,
           lambda_q1, lambda_k1, lambda_q2, lambda_k2, subln_w, gate_b,
           w_proj_ssm, w_proj_attn, w_out, final_norm_w):
    batch, seq, _ = x.shape
    depth = norm_w.shape[0]
    assert seq % CHUNK == 0
    rope = _rope_tables(seq)
    x2 = x.reshape(batch * seq, D_MODEL)
    head_col = lambda v: v.astype(F32).reshape(SSM_HEADS, 1)

    dt_off = D_INNER + XBC_DIM
    for l in range(depth):
        lambda_init = 0.8 - 0.6 * math.exp(-0.3 * l)
        w = w_in[l]
        w_main = jnp.concatenate(
            [w[:, D_INNER:2 * D_INNER], w[:, :D_INNER], w[:, 2 * D_INNER:dt_off],
             w[:, dt_off + SSM_HEADS:]], axis=1).astype(BF16)
        w_dt = jnp.pad(w[:, dt_off:dt_off + SSM_HEADS], ((0, 0), (0, DT_PAD - SSM_HEADS)))
        wdt_hi = w_dt.astype(BF16)
        wdt_lo = (w_dt - wdt_hi.astype(F32)).astype(BF16)

        main, dt_raw = _inproj(x2, norm_w[l].reshape(1, D_MODEL), w_main, wdt_hi, wdt_lo, seq)

        y_ssm = _ssd(
            main, dt_raw,
            conv_w[l][:, :D_INNER], conv_b[l][:D_INNER].reshape(1, D_INNER),
            conv_w[l][:, D_INNER:], conv_b[l][D_INNER:].reshape(1, BC_DIM),
            head_col(dt_bias[l]), head_col(a_log[l]),
            jnp.repeat(d_skip[l].astype(F32), SSM_HEAD_DIM).reshape(1, D_INNER),
            ssm_norm_w[l].reshape(1, D_INNER), batch, seq)

        y_attn = _attention(
            main, rope, lambda_q1[l].reshape(1, -1), lambda_k1[l].reshape(1, -1),
            lambda_q2[l].reshape(1, -1), lambda_k2[l].reshape(1, -1),
            subln_w[l].reshape(ATTN_V_DIM, 1), batch, seq, lambda_init)

        x2 = _merge(x2, y_ssm, y_attn, main, gate_b[l].reshape(1, 2 * D_MODEL),
                    w_proj_ssm[l].astype(BF16), w_proj_attn[l].astype(BF16),
                    w_out[l].astype(BF16), final_norm_w.reshape(1, D_MODEL),
                    final_norm=(l == depth - 1))
    return x2.reshape(batch, seq, D_MODEL)
```

```python
import functools
import math

import jax
import jax.numpy as jnp
from jax import lax
from jax.experimental import pallas as pl
from jax.experimental.pallas import tpu as pltpu

F32 = jnp.float32
BF16 = jnp.bfloat16

D_MODEL = 1024
EPS = 1e-5
D_INNER = 2 * D_MODEL
SSM_HEAD_DIM = 64
SSM_HEADS = D_INNER // SSM_HEAD_DIM
SSM_GROUPS = 4
HEADS_PER_GROUP = SSM_HEADS // SSM_GROUPS
D_STATE = 128
CONV_WIDTH = 4
CHUNK = 128
BC_DIM = 2 * SSM_GROUPS * D_STATE
XBC_DIM = D_INNER + BC_DIM
ATTN_HEADS = 8
ATTN_HEAD_DIM = 64
ATTN_V_DIM = 2 * ATTN_HEAD_DIM
ATTN_WIDTH = ATTN_HEADS * ATTN_V_DIM
ROPE_THETA = 10000.0
LOG2_E = math.log2(math.e)
ATTN_BLOCK = 512
VT_ROWS = ATTN_V_DIM + 16

LANES = 128
PREV_ROWS = 16
CONV_SHIFT_ROWS = (CONV_WIDTH - 1) * CHUNK

COL_TILE = 1024
N_COL_TILES = 11
MAIN_DIM = COL_TILE * N_COL_TILES
Q_TILE = 5
K_TILE = 6
DT_PAD = LANES

INPROJ_ROWS = 2048
VMEM_LIMIT = 56 * 1024 * 1024


def _silu(x):
    return x * (1.0 / (1.0 + jnp.exp2(x * -LOG2_E)))


def _dot(a, b):
    return jnp.dot(a, b, preferred_element_type=F32)


def _dot_nt(a, b):
    return lax.dot_general(a, b, (((1,), (1,)), ((), ())), preferred_element_type=F32)


def _inproj_kernel(x_ref, nw_ref, w_ref, wdt_hi_ref, wdt_lo_ref, out_ref, dt_ref, h_ref):
    @pl.when(pl.program_id(1) == 0)
    def _():
        x = x_ref[...]
        var = jnp.mean(x * x, axis=-1, keepdims=True)
        h = x * lax.rsqrt(var + EPS) * nw_ref[...]
        h_hi = h.astype(BF16)
        h_ref[...] = h_hi
        h_lo = (h - h_hi.astype(F32)).astype(BF16)
        dt_ref[...] = (_dot(h_hi, wdt_hi_ref[...]) + _dot(h_lo, wdt_hi_ref[...])
                       + _dot(h_hi, wdt_lo_ref[...]))

    out_ref[...] = _dot(h_ref[...], w_ref[...]).astype(out_ref.dtype)


def _inproj(x2, norm_w, w_main, wdt_hi, wdt_lo, seq):
    tokens = x2.shape[0]
    tm = min(INPROJ_ROWS, seq)
    return pl.pallas_call(
        _inproj_kernel,
        grid=(tokens // tm, N_COL_TILES),
        in_specs=[
            pl.BlockSpec((tm, D_MODEL), lambda i, j: (i, 0)),
            pl.BlockSpec((1, D_MODEL), lambda i, j: (0, 0)),
            pl.BlockSpec((D_MODEL, COL_TILE), lambda i, j: (0, j)),
            pl.BlockSpec((D_MODEL, DT_PAD), lambda i, j: (0, 0)),
            pl.BlockSpec((D_MODEL, DT_PAD), lambda i, j: (0, 0)),
        ],
        out_specs=[
            pl.BlockSpec((tm, COL_TILE), lambda i, j: (i, j)),
            pl.BlockSpec((tm, DT_PAD), lambda i, j: (i, 0)),
        ],
        out_shape=[
            jax.ShapeDtypeStruct((tokens, MAIN_DIM), BF16),
            jax.ShapeDtypeStruct((tokens, DT_PAD), F32),
        ],
        scratch_shapes=[pltpu.VMEM((tm, D_MODEL), BF16)],
        compiler_params=pltpu.CompilerParams(
            dimension_semantics=("parallel", "arbitrary"), vmem_limit_bytes=VMEM_LIMIT),
        name="inproj",
    )(x2, norm_w, w_main, wdt_hi, wdt_lo)


def _split3(v):
    hi = v.astype(BF16)
    r = v - hi.astype(F32)
    mid = r.astype(BF16)
    lo = (r - mid.astype(F32)).astype(BF16)
    return hi, mid, lo


def _conv_silu(u_ref, prev_ref, shift, w_ref, b_ref):
    u = u_ref[...]
    shifted = _dot(shift, jnp.concatenate([prev_ref[...], u], axis=0))
    out = b_ref[...] + w_ref[CONV_WIDTH - 1:CONV_WIDTH, :] * u.astype(F32)
    for k in range(1, CONV_WIDTH):
        out = out + w_ref[CONV_WIDTH - 1 - k:CONV_WIDTH - k, :] * shifted[(k - 1) * CHUNK:k * CHUNK, :]
    return _silu(out)


def _pad_rows_t(v_t):
    pad = jnp.zeros((LANES - v_t.shape[0], v_t.shape[1]), F32)
    return jnp.concatenate([v_t, pad], axis=0).T


def _split3_t(v_t):
    hi, mid, lo = _split3(v_t)
    return _pad_rows_t(jnp.concatenate(
        [hi.astype(F32), mid.astype(F32), lo.astype(F32)], axis=0)).astype(BF16)


def _ssd_kernel(xs_ref, z_ref, bc_ref, xs_prev_ref, bc_prev_ref, dt_ref, cwx_ref, cbx_ref,
                cwbc_ref, cbbc_ref, dtb_ref, alog_ref, dskip_ref, nw_ref, shift_ref, expand_ref,
                y_ref, state_ref):
    first = pl.program_id(1) == 0

    @pl.when(first)
    def _():
        state_ref[...] = jnp.zeros_like(state_ref)

    shift = shift_ref[jnp.where(first, 1, 0)]
    xs = _conv_silu(xs_ref, xs_prev_ref, shift, cwx_ref, cbx_ref)
    bc = _conv_silu(bc_ref, bc_prev_ref, shift, cwbc_ref, cbbc_ref)

    dt_in = dt_ref[...].T[:SSM_HEADS] + dtb_ref[...]
    dt = jnp.maximum(dt_in, 0.0) + jnp.log1p(jnp.exp(-jnp.abs(dt_in)))
    adt = dt * (-jnp.exp(alog_ref[...]))

    row = lax.broadcasted_iota(jnp.int32, (CHUNK, CHUNK), 0)
    col = lax.broadcasted_iota(jnp.int32, (CHUNK, CHUNK), 1)
    tril = row >= col
    triu_bf = (row <= col).astype(BF16)
    hi, mid, lo = _split3(adt)
    a_cs_t = _dot(hi, triu_bf) + _dot(mid, triu_bf) + _dot(lo, triu_bf)
    key_t = a_cs_t - jnp.log(dt)
    a_cs = _pad_rows_t(a_cs_t)
    e_acs_t = jnp.exp(a_cs_t)
    w_end_t = dt * jnp.exp(a_cs_t[:, CHUNK - 1:CHUNK] - a_cs_t)
    expanded = _dot(jnp.concatenate([_split3_t(e_acs_t), _split3_t(w_end_t)], axis=0),
                    expand_ref[...])
    e_acs_full = expanded[:CHUNK]
    xs_bf = xs.astype(BF16)
    xw_bf = (xs * expanded[CHUNK:]).astype(BF16)

    gw = D_INNER // SSM_GROUPS
    pair_w = 2 * SSM_HEAD_DIM
    lane = lax.broadcasted_iota(jnp.int32, (CHUNK, pair_w), 1)
    y_parts = []
    for g in range(SSM_GROUPS):
        b_g = bc[:, g * D_STATE:(g + 1) * D_STATE]
        c_bf = bc[:, BC_DIM // 2 + g * D_STATE:BC_DIM // 2 + (g + 1) * D_STATE].astype(BF16)
        cb = _dot_nt(c_bf, b_g.astype(BF16))
        s_g = state_ref[g]
        cols = slice(g * gw, (g + 1) * gw)
        y_off = _dot(c_bf, s_g.astype(BF16)) * e_acs_full[:, cols]
        for pr in range(HEADS_PER_GROUP // 2):
            h0 = g * HEADS_PER_GROUP + 2 * pr
            x_pair = xs_bf[:, h0 * SSM_HEAD_DIM:(h0 + 2) * SSM_HEAD_DIM]
            ys = []
            for h in (h0, h0 + 1):
                seg = a_cs[:, h:h + 1] - key_t[h:h + 1, :]
                m = (cb * jnp.exp(jnp.where(tril, seg, -jnp.inf))).astype(BF16)
                ys.append(_dot(m, x_pair))
            y_parts.append(jnp.where(lane < SSM_HEAD_DIM, ys[0], ys[1])
                           + y_off[:, pr * pair_w:(pr + 1) * pair_w])
        state_ref[g] = (s_g * e_acs_full[CHUNK - 1:CHUNK, cols]
                        + _dot(b_g.T.astype(BF16), xw_bf[:, cols]))

    y = jnp.concatenate(y_parts, axis=1) + xs * dskip_ref[...]
    y = y * _silu(z_ref[...].astype(F32))
    for g in range(SSM_GROUPS):
        yg = y[:, g * gw:(g + 1) * gw]
        var = jnp.mean(yg * yg, axis=-1, keepdims=True)
        y_ref[:, g * gw:(g + 1) * gw] = (
            yg * lax.rsqrt(var + EPS) * nw_ref[:, g * gw:(g + 1) * gw]).astype(y_ref.dtype)


def _ssd_constants():
    t = jnp.arange(CONV_SHIFT_ROWS)[:, None]
    j = jnp.arange(CHUNK + PREV_ROWS)[None, :]
    shift = (j == PREV_ROWS + (t % CHUNK) - (t // CHUNK + 1))
    shift = jnp.stack([shift, shift & (j >= PREV_ROWS)]).astype(BF16)
    r = jnp.arange(LANES)[:, None]
    c = jnp.arange(D_INNER)[None, :]
    expand = ((r < 3 * SSM_HEADS) & (r % SSM_HEADS == c // SSM_HEAD_DIM)).astype(BF16)
    return shift, expand


def _ssd(main, dt_raw, cwx, cbx, cwbc, cbbc, dtb, alog, dskip, nw, batch, seq):
    nc = seq // CHUNK
    tokens = batch * seq
    shift, expand = _ssd_constants()
    row_idx = lambda b, c: b * nc + c
    prev_idx = lambda b, c: jnp.maximum((b * nc + c) * (CHUNK // PREV_ROWS) - 1, 0)
    const = lambda b, c: (0, 0)
    return pl.pallas_call(
        _ssd_kernel,
        grid=(batch, nc),
        in_specs=[
            pl.BlockSpec((CHUNK, D_INNER), lambda b, c: (row_idx(b, c), 0)),
            pl.BlockSpec((CHUNK, D_INNER), lambda b, c: (row_idx(b, c), 1)),
            pl.BlockSpec((CHUNK, BC_DIM), lambda b, c: (row_idx(b, c), 4)),
            pl.BlockSpec((PREV_ROWS, D_INNER), lambda b, c: (prev_idx(b, c), 0)),
            pl.BlockSpec((PREV_ROWS, BC_DIM), lambda b, c: (prev_idx(b, c), 4)),
            pl.BlockSpec((CHUNK, DT_PAD), lambda b, c: (row_idx(b, c), 0)),
            pl.BlockSpec((CONV_WIDTH, D_INNER), const),
            pl.BlockSpec((1, D_INNER), const),
            pl.BlockSpec((CONV_WIDTH, BC_DIM), const),
            pl.BlockSpec((1, BC_DIM), const),
            pl.BlockSpec((SSM_HEADS, 1), const),
            pl.BlockSpec((SSM_HEADS, 1), const),
            pl.BlockSpec((1, D_INNER), const),
            pl.BlockSpec((1, D_INNER), const),
            pl.BlockSpec((2, CONV_SHIFT_ROWS, CHUNK + PREV_ROWS), lambda b, c: (0, 0, 0)),
            pl.BlockSpec((LANES, D_INNER), const),
        ],
        out_specs=pl.BlockSpec((CHUNK, D_INNER), lambda b, c: (row_idx(b, c), 0)),
        out_shape=jax.ShapeDtypeStruct((tokens, D_INNER), BF16),
        scratch_shapes=[
            pltpu.VMEM((SSM_GROUPS, D_STATE, D_INNER // SSM_GROUPS), F32),
        ],
        compiler_params=pltpu.CompilerParams(
            dimension_semantics=("parallel", "arbitrary"), vmem_limit_bytes=VMEM_LIMIT),
        name="ssd",
    )(main, main, main, main, main, dt_raw, cwx, cbx, cwbc, cbbc, dtb, alog, dskip, nw, shift, expand)


def _rotate_half_rows(x_t):
    half = ATTN_HEAD_DIM // 2
    parts = []
    for m in range(2):
        base = m * ATTN_HEAD_DIM
        parts += [x_t[base + half:base + 2 * half], x_t[base:base + half]]
    return jnp.concatenate(parts, axis=0)


def _rotate_half_lanes(x):
    half = ATTN_HEAD_DIM // 2
    lane = lax.broadcasted_iota(jnp.int32, x.shape, 1)
    return jnp.where((lane % ATTN_HEAD_DIM) < half,
                     pltpu.roll(x, LANES - half, 1), pltpu.roll(x, half, 1))


def _attn_kernel(q_ref, k_ref, v_ref, z_ref, cosq_ref, sinq_ref, cosk_ref, sink_ref,
                 lq1_ref, lk1_ref, lq2_ref, lk2_ref, sw_ref,
                 o_ref, vt_ref, kr_ref, *scratch, tq, tk, lambda_init):
    ones_rows = (lax.broadcasted_iota(jnp.int32, (VT_ROWS - ATTN_V_DIM, tk), 0) == 0).astype(BF16)
    for c in range(vt_ref.shape[0]):
        rows = slice(c * tk, (c + 1) * tk)
        vt_ref[c, :ATTN_V_DIM, :] = v_ref[rows, :].astype(F32).T.astype(BF16)
        vt_ref[c, ATTN_V_DIM:, :] = ones_rows
        k = k_ref[rows, :].astype(F32)
        kr_ref[rows, :] = (k * cosk_ref[rows, :]
                           + _rotate_half_lanes(k) * sink_ref[rows, :]).astype(BF16)

    def qblock(qi, carry):
        _attn_qblock(qi, q_ref, z_ref, cosq_ref, sinq_ref, lq1_ref, lk1_ref, lq2_ref, lk2_ref, sw_ref,
                     o_ref, vt_ref, kr_ref, *scratch, tq=tq, tk=tk, lambda_init=lambda_init)
        return carry

    lax.fori_loop(0, q_ref.shape[0] // tq, qblock, 0)


def _attn_qblock(qi, q_ref, z_ref, cosq_ref, sinq_ref, lq1_ref, lk1_ref, lq2_ref, lk2_ref, sw_ref,
                 o_ref, vt_ref, kr_ref, qq_ref, s0_ref, s1_ref, mb0_ref, mb1_ref, p0_ref, p1_ref,
                 a0_ref, a1_ref, m_ref, acc_ref, *, tq, tk, lambda_init):
    q_rows = pl.ds(pl.multiple_of(qi * tq, tq), tq)
    q_t = q_ref[q_rows, :].astype(F32).T
    q_t = q_t * cosq_ref[qi] + _rotate_half_rows(q_t) * sinq_ref[qi]
    row = lax.broadcasted_iota(jnp.int32, q_t.shape, 0)
    qq_ref[...] = jnp.concatenate([jnp.where(row < ATTN_HEAD_DIM, q_t, 0.0),
                                   jnp.where(row >= ATTN_HEAD_DIM, q_t, 0.0)], axis=1).astype(BF16)

    m_ref[...] = jnp.full_like(m_ref, -jnp.inf)
    acc_ref[...] = jnp.zeros_like(acc_ref)
    p1_ref[...] = jnp.zeros_like(p1_ref)
    a1_ref[...] = jnp.ones_like(a1_ref)

    def scores(j, s_ref, mb_ref):
        start = j * tk if isinstance(j, int) else pl.multiple_of(j * tk, tk)
        s = _dot(kr_ref[pl.ds(start, tk), :], qq_ref[...])
        s_ref[...] = s
        mb_ref[...] = jnp.max(s, axis=0, keepdims=True)

    def softmax(s_ref, mb_ref, p_ref, a_ref, diag_half=None):
        s = s_ref[...]
        if diag_half is None:
            mb = mb_ref[...]
        else:
            key = lax.broadcasted_iota(jnp.int32, s.shape, 0) + diag_half * tk
            qry = lax.broadcasted_iota(jnp.int32, s.shape, 1) & (tq - 1)
            s = jnp.where(key <= qry, s, -jnp.inf)
            mb = jnp.max(s, axis=0, keepdims=True)
        m_prev = m_ref[...]
        m_new = jnp.maximum(m_prev, mb)
        a_ref[...] = jnp.exp2(m_prev - m_new)
        p_ref[...] = jnp.exp2((s - m_new).astype(BF16))
        m_ref[...] = m_new

    def value(j, p_ref, a_ref):
        acc_ref[...] = acc_ref[...] * a_ref[...] + _dot(vt_ref[jnp.maximum(j, 0)], p_ref[...])

    scores(0, s0_ref, mb0_ref)

    def body(k, carry):
        softmax(s0_ref, mb0_ref, p0_ref, a0_ref)
        value(2 * k - 1, p1_ref, a1_ref)
        scores(2 * k + 1, s1_ref, mb1_ref)
        softmax(s1_ref, mb1_ref, p1_ref, a1_ref)
        value(2 * k, p0_ref, a0_ref)
        scores(2 * k + 2, s0_ref, mb0_ref)
        return carry

    lax.fori_loop(0, qi, body, 0)
    scores(2 * qi + 1, s1_ref, mb1_ref)
    softmax(s0_ref, mb0_ref, p0_ref, a0_ref, diag_half=0)
    value(2 * qi - 1, p1_ref, a1_ref)
    softmax(s1_ref, mb1_ref, p1_ref, a1_ref, diag_half=1)
    value(2 * qi, p0_ref, a0_ref)
    value(2 * qi + 1, p1_ref, a1_ref)

    lam = (jnp.exp(jnp.sum(lq1_ref[...] * lk1_ref[...], axis=-1, keepdims=True))
           - jnp.exp(jnp.sum(lq2_ref[...] * lk2_ref[...], axis=-1, keepdims=True))
           + lambda_init)
    o_maps = acc_ref[:ATTN_V_DIM, :] * (1.0 / acc_ref[ATTN_V_DIM:ATTN_V_DIM + 1, :])
    o = o_maps[:, :tq] - lam * o_maps[:, tq:]
    var = jnp.mean(o * o, axis=0, keepdims=True)
    o = o * lax.rsqrt(var + EPS) * sw_ref[...] * (1.0 - lambda_init)
    o_ref[q_rows, :] = (o.T * _silu(z_ref[q_rows, :].astype(F32))).astype(o_ref.dtype)


def _attention(main, rope, lq1, lk1, lq2, lk2, subln_w, batch, seq, lambda_init):
    tq = min(ATTN_BLOCK, seq)
    tk = tq // 2
    assert tq & (tq - 1) == 0 and seq % tq == 0
    nq = seq // tq
    tokens = batch * seq
    q_col = Q_TILE * COL_TILE // ATTN_V_DIM
    k_col = K_TILE * COL_TILE // ATTN_V_DIM
    v_col = (K_TILE + 1) * COL_TILE // ATTN_V_DIM
    z_col = (K_TILE + 2) * COL_TILE // ATTN_V_DIM
    const = lambda b, h: (0, 0)
    cos_q, sin_q, cos_k, sin_k = rope
    cos_q = cos_q.reshape(ATTN_V_DIM, nq, tq).transpose(1, 0, 2)
    sin_q = sin_q.reshape(ATTN_V_DIM, nq, tq).transpose(1, 0, 2)
    return pl.pallas_call(
        functools.partial(_attn_kernel, tq=tq, tk=tk, lambda_init=lambda_init),
        grid=(batch, ATTN_HEADS),
        in_specs=[
            pl.BlockSpec((seq, ATTN_V_DIM), lambda b, h: (b, q_col + h)),
            pl.BlockSpec((seq, ATTN_V_DIM), lambda b, h: (b, k_col + h)),
            pl.BlockSpec((seq, ATTN_V_DIM), lambda b, h: (b, v_col + h)),
            pl.BlockSpec((seq, ATTN_V_DIM), lambda b, h: (b, z_col + h)),
            pl.BlockSpec((nq, ATTN_V_DIM, tq), lambda b, h: (0, 0, 0)),
            pl.BlockSpec((nq, ATTN_V_DIM, tq), lambda b, h: (0, 0, 0)),
            pl.BlockSpec((seq, ATTN_V_DIM), const),
            pl.BlockSpec((seq, ATTN_V_DIM), const),
            pl.BlockSpec((1, ATTN_HEAD_DIM), const),
            pl.BlockSpec((1, ATTN_HEAD_DIM), const),
            pl.BlockSpec((1, ATTN_HEAD_DIM), const),
            pl.BlockSpec((1, ATTN_HEAD_DIM), const),
            pl.BlockSpec((ATTN_V_DIM, 1), const),
        ],
        out_specs=pl.BlockSpec((seq, ATTN_V_DIM), lambda b, h: (b, h)),
        out_shape=jax.ShapeDtypeStruct((tokens, ATTN_WIDTH), BF16),
        scratch_shapes=[
            pltpu.VMEM((seq // tk, VT_ROWS, tk), BF16),
            pltpu.VMEM((seq, ATTN_V_DIM), BF16),
            pltpu.VMEM((ATTN_V_DIM, 2 * tq), BF16),
            pltpu.VMEM((tk, 2 * tq), F32),
            pltpu.VMEM((tk, 2 * tq), F32),
            pltpu.VMEM((1, 2 * tq), F32),
            pltpu.VMEM((1, 2 * tq), F32),
            pltpu.VMEM((tk, 2 * tq), BF16),
            pltpu.VMEM((tk, 2 * tq), BF16),
            pltpu.VMEM((1, 2 * tq), F32),
            pltpu.VMEM((1, 2 * tq), F32),
            pltpu.VMEM((1, 2 * tq), F32),
            pltpu.VMEM((VT_ROWS, 2 * tq), F32),
        ],
        compiler_params=pltpu.CompilerParams(
            dimension_semantics=("parallel", "parallel"),
            vmem_limit_bytes=VMEM_LIMIT),
        name="diff_attn",
    )(main, main, main, main, cos_q, sin_q, cos_k, sin_k, lq1, lk1, lq2, lk2, subln_w)


def _merge_kernel(x_ref, ys_ref, ya_ref, gs_ref, ga_ref, gb_ref, ps_ref, pa_ref, wo_ref, fw_ref,
                  o_ref, *, final_norm):
    proj_s = _dot(ys_ref[...], ps_ref[...])
    proj_a = _dot(ya_ref[...], pa_ref[...])
    g_s = 1.0 / (1.0 + jnp.exp(-(gs_ref[...].astype(F32) + gb_ref[:, :D_MODEL])))
    g_a = 1.0 / (1.0 + jnp.exp(-(ga_ref[...].astype(F32) + gb_ref[:, D_MODEL:])))
    merged = g_s * proj_s + g_a * proj_a
    x = x_ref[...] + _dot(merged.astype(BF16), wo_ref[...])
    if final_norm:
        var = jnp.mean(x * x, axis=-1, keepdims=True)
        x = x * lax.rsqrt(var + EPS) * fw_ref[...]
    o_ref[...] = x


def _merge(x2, y_ssm, y_attn, main, gate_b, p_ssm, p_attn, w_out, final_w, final_norm):
    tokens = x2.shape[0]
    tm = min(512, tokens)
    const = lambda i: (0, 0)
    return pl.pallas_call(
        functools.partial(_merge_kernel, final_norm=final_norm),
        grid=(tokens // tm,),
        in_specs=[
            pl.BlockSpec((tm, D_MODEL), lambda i: (i, 0)),
            pl.BlockSpec((tm, D_INNER), lambda i: (i, 0)),
            pl.BlockSpec((tm, ATTN_WIDTH), lambda i: (i, 0)),
            pl.BlockSpec((tm, D_MODEL), lambda i: (i, N_COL_TILES - 2)),
            pl.BlockSpec((tm, D_MODEL), lambda i: (i, N_COL_TILES - 1)),
            pl.BlockSpec((1, 2 * D_MODEL), const),
            pl.BlockSpec((D_INNER, D_MODEL), const),
            pl.BlockSpec((ATTN_WIDTH, D_MODEL), const),
            pl.BlockSpec((D_MODEL, D_MODEL), const),
            pl.BlockSpec((1, D_MODEL), const),
        ],
        out_specs=pl.BlockSpec((tm, D_MODEL), lambda i: (i, 0)),
        out_shape=jax.ShapeDtypeStruct((tokens, D_MODEL), F32),
        compiler_params=pltpu.CompilerParams(
            dimension_semantics=("parallel",), vmem_limit_bytes=VMEM_LIMIT),
        name="merge",
    )(x2, y_ssm, y_attn, main, main, gate_b, p_ssm, p_attn, w_out, final_w)


def _rope_tables(seq):
    inv = 1.0 / (ROPE_THETA ** (jnp.arange(0, ATTN_HEAD_DIM, 2, dtype=F32) / ATTN_HEAD_DIM))
    f = jnp.arange(seq, dtype=F32)[:, None] * inv[None, :]
    cos = jnp.cos(f)
    sin = jnp.sin(f)
    cos_k = jnp.concatenate([cos, cos, cos, cos], axis=-1)
    sin_k = jnp.concatenate([-sin, sin, -sin, sin], axis=-1)
    q_scale = ATTN_HEAD_DIM ** -0.5 * LOG2_E
    return (cos_k.T * q_scale, sin_k.T * q_scale, cos_k, sin_k)


def kernel(x, norm_w, w_in, conv_w, conv_b, dt_bias, a_log, d_skip, ssm_norm_w,
           lambda_q1, lambda_k1, lambda_q2, lambda_k2, subln_w, gate_b,
           w_proj_ssm, w_proj_attn, w_out, final_norm_w):
    batch, seq, _ = x.shape
    depth = norm_w.shape[0]
    assert seq % CHUNK == 0
    rope = _rope_tables(seq)
    x2 = x.reshape(batch * seq, D_MODEL)
    head_col = lambda v: v.astype(F32).reshape(SSM_HEADS, 1)

    dt_off = D_INNER + XBC_DIM
    for l in range(depth):
        lambda_init = 0.8 - 0.6 * math.exp(-0.3 * l)
        w = w_in[l]
        w_main = jnp.concatenate(
            [w[:, D_INNER:2 * D_INNER], w[:, :D_INNER], w[:, 2 * D_INNER:dt_off],
             w[:, dt_off + SSM_HEADS:]], axis=1).astype(BF16)
        w_dt = jnp.pad(w[:, dt_off:dt_off + SSM_HEADS], ((0, 0), (0, DT_PAD - SSM_HEADS)))
        wdt_hi = w_dt.astype(BF16)
        wdt_lo = (w_dt - wdt_hi.astype(F32)).astype(BF16)

        main, dt_raw = _inproj(x2, norm_w[l].reshape(1, D_MODEL), w_main, wdt_hi, wdt_lo, seq)

        y_ssm = _ssd(
            main, dt_raw,
            conv_w[l][:, :D_INNER], conv_b[l][:D_INNER].reshape(1, D_INNER),
            conv_w[l][:, D_INNER:], conv_b[l][D_INNER:].reshape(1, BC_DIM),
            head_col(dt_bias[l]), head_col(a_log[l]),
            jnp.repeat(d_skip[l].astype(F32), SSM_HEAD_DIM).reshape(1, D_INNER),
            ssm_norm_w[l].reshape(1, D_INNER), batch, seq)

        y_attn = _attention(
            main, rope, lambda_q1[l].reshape(1, -1), lambda_k1[l].reshape(1, -1),
            lambda_q2[l].reshape(1, -1), lambda_k2[l].reshape(1, -1),
            subln_w[l].reshape(ATTN_V_DIM, 1), batch, seq, lambda_init)

        x2 = _merge(x2, y_ssm, y_attn, main, gate_b[l].reshape(1, 2 * D_MODEL),
                    w_proj_ssm[l].astype(BF16), w_proj_attn[l].astype(BF16),
                    w_out[l].astype(BF16), final_norm_w.reshape(1, D_MODEL),
                    final_norm=(l == depth - 1))
    return x2.reshape(batch, seq, D_MODEL)
```

```python
import functools
import math

import jax
import jax.numpy as jnp
from jax import lax
from jax.experimental import pallas as pl
from jax.experimental.pallas import tpu as pltpu

F32 = jnp.float32
BF16 = jnp.bfloat16

D_MODEL = 1024
EPS = 1e-5
D_INNER = 2 * D_MODEL
SSM_HEAD_DIM = 64
SSM_HEADS = D_INNER // SSM_HEAD_DIM
SSM_GROUPS = 4
HEADS_PER_GROUP = SSM_HEADS // SSM_GROUPS
D_STATE = 128
CONV_WIDTH = 4
CHUNK = 128
BC_DIM = 2 * SSM_GROUPS * D_STATE
XBC_DIM = D_INNER + BC_DIM
ATTN_HEADS = 8
ATTN_HEAD_DIM = 64
ATTN_V_DIM = 2 * ATTN_HEAD_DIM
ATTN_WIDTH = ATTN_HEADS * ATTN_V_DIM
ROPE_THETA = 10000.0
LOG2_E = math.log2(math.e)
ATTN_BLOCK = 512
VT_ROWS = ATTN_V_DIM + 16

LANES = 128
PREV_ROWS = 16
CONV_SHIFT_ROWS = (CONV_WIDTH - 1) * CHUNK

COL_TILE = 1024
N_COL_TILES = 11
MAIN_DIM = COL_TILE * N_COL_TILES
Q_TILE = 5
K_TILE = 6
DT_PAD = LANES

INPROJ_ROWS = 2048
SSD_ROWS = 4 * CHUNK
VMEM_LIMIT = 56 * 1024 * 1024


def _silu(x):
    return x * (1.0 / (1.0 + jnp.exp2(x * -LOG2_E)))


def _dot(a, b):
    return jnp.dot(a, b, preferred_element_type=F32)


def _dot_nt(a, b):
    return lax.dot_general(a, b, (((1,), (1,)), ((), ())), preferred_element_type=F32)


def _inproj_kernel(x_ref, nw_ref, w_ref, wdt_hi_ref, wdt_lo_ref, out_ref, dt_ref, h_ref):
    @pl.when(pl.program_id(1) == 0)
    def _():
        x = x_ref[...]
        var = jnp.mean(x * x, axis=-1, keepdims=True)
        h = x * lax.rsqrt(var + EPS) * nw_ref[...]
        h_hi = h.astype(BF16)
        h_ref[...] = h_hi
        h_lo = (h - h_hi.astype(F32)).astype(BF16)
        dt_ref[...] = (_dot(h_hi, wdt_hi_ref[...]) + _dot(h_lo, wdt_hi_ref[...])
                       + _dot(h_hi, wdt_lo_ref[...]))

    out_ref[...] = _dot(h_ref[...], w_ref[...]).astype(out_ref.dtype)


def _inproj(x2, norm_w, w_main, wdt_hi, wdt_lo, seq):
    tokens = x2.shape[0]
    tm = min(INPROJ_ROWS, seq)
    return pl.pallas_call(
        _inproj_kernel,
        grid=(tokens // tm, N_COL_TILES),
        in_specs=[
            pl.BlockSpec((tm, D_MODEL), lambda i, j: (i, 0)),
            pl.BlockSpec((1, D_MODEL), lambda i, j: (0, 0)),
            pl.BlockSpec((D_MODEL, COL_TILE), lambda i, j: (0, j)),
            pl.BlockSpec((D_MODEL, DT_PAD), lambda i, j: (0, 0)),
            pl.BlockSpec((D_MODEL, DT_PAD), lambda i, j: (0, 0)),
        ],
        out_specs=[
            pl.BlockSpec((tm, COL_TILE), lambda i, j: (i, j)),
            pl.BlockSpec((tm, DT_PAD), lambda i, j: (i, 0)),
        ],
        out_shape=[
            jax.ShapeDtypeStruct((tokens, MAIN_DIM), BF16),
            jax.ShapeDtypeStruct((tokens, DT_PAD), F32),
        ],
        scratch_shapes=[pltpu.VMEM((tm, D_MODEL), BF16)],
        compiler_params=pltpu.CompilerParams(
            dimension_semantics=("parallel", "arbitrary"), vmem_limit_bytes=VMEM_LIMIT),
        name="inproj",
    )(x2, norm_w, w_main, wdt_hi, wdt_lo)


def _split3(v):
    hi = v.astype(BF16)
    r = v - hi.astype(F32)
    mid = r.astype(BF16)
    lo = (r - mid.astype(F32)).astype(BF16)
    return hi, mid, lo


def _conv_silu(u, prev, shift, w_ref, b_ref):
    shifted = _dot(shift, jnp.concatenate([prev, u], axis=0))
    out = b_ref[...] + w_ref[CONV_WIDTH - 1:CONV_WIDTH, :] * u.astype(F32)
    for k in range(1, CONV_WIDTH):
        out = out + w_ref[CONV_WIDTH - 1 - k:CONV_WIDTH - k, :] * shifted[(k - 1) * CHUNK:k * CHUNK, :]
    return _silu(out)


def _pad_rows_t(v_t):
    pad = jnp.zeros((LANES - v_t.shape[0], v_t.shape[1]), F32)
    return jnp.concatenate([v_t, pad], axis=0).T


def _split3_t(v_t):
    hi, mid, lo = _split3(v_t)
    return _pad_rows_t(jnp.concatenate(
        [hi.astype(F32), mid.astype(F32), lo.astype(F32)], axis=0)).astype(BF16)


def _ssd_kernel(xs_ref, z_ref, bc_ref, xs_prev_ref, bc_prev_ref, dt_ref, cwx_ref, cbx_ref,
                cwbc_ref, cbbc_ref, dtb_ref, alog_ref, dskip_ref, nw_ref, shift_ref, expand_ref,
                y_ref, state_ref):
    first = pl.program_id(1) == 0

    @pl.when(first)
    def _():
        state_ref[...] = jnp.zeros_like(state_ref)

    for sub in range(xs_ref.shape[0] // CHUNK):
        rows = slice(sub * CHUNK, (sub + 1) * CHUNK)
        if sub == 0:
            shift = shift_ref[jnp.where(first, 1, 0)]
            xs_prev, bc_prev = xs_prev_ref[...], bc_prev_ref[...]
        else:
            shift = shift_ref[0]
            hist = slice(sub * CHUNK - PREV_ROWS, sub * CHUNK)
            xs_prev, bc_prev = xs_ref[hist, :], bc_ref[hist, :]
        _ssd_chunk(xs_ref[rows, :], z_ref[rows, :], bc_ref[rows, :], xs_prev, bc_prev, dt_ref[rows, :],
                   shift, cwx_ref, cbx_ref, cwbc_ref, cbbc_ref, dtb_ref, alog_ref, dskip_ref, nw_ref,
                   expand_ref, y_ref.at[rows, :], state_ref)


def _ssd_chunk(xs_raw, z_raw, bc_raw, xs_prev, bc_prev, dt_raw, shift, cwx_ref, cbx_ref, cwbc_ref, cbbc_ref,
               dtb_ref, alog_ref, dskip_ref, nw_ref, expand_ref, y_ref, state_ref):
    xs = _conv_silu(xs_raw, xs_prev, shift, cwx_ref, cbx_ref)
    bc = _conv_silu(bc_raw, bc_prev, shift, cwbc_ref, cbbc_ref)

    dt_in = dt_raw.T[:SSM_HEADS] + dtb_ref[...]
    dt = jnp.maximum(dt_in, 0.0) + jnp.log1p(jnp.exp(-jnp.abs(dt_in)))
    adt = dt * (-jnp.exp(alog_ref[...]))

    row = lax.broadcasted_iota(jnp.int32, (CHUNK, CHUNK), 0)
    col = lax.broadcasted_iota(jnp.int32, (CHUNK, CHUNK), 1)
    tril = row >= col
    triu_bf = (row <= col).astype(BF16)
    hi, mid, lo = _split3(adt)
    a_cs_t = _dot(hi, triu_bf) + _dot(mid, triu_bf) + _dot(lo, triu_bf)
    key_t = a_cs_t - jnp.log(dt)
    a_cs = _pad_rows_t(a_cs_t)
    e_acs_t = jnp.exp(a_cs_t)
    w_end_t = dt * jnp.exp(a_cs_t[:, CHUNK - 1:CHUNK] - a_cs_t)
    expanded = _dot(jnp.concatenate([_split3_t(e_acs_t), _split3_t(w_end_t)], axis=0),
                    expand_ref[...])
    e_acs_full = expanded[:CHUNK]
    xs_bf = xs.astype(BF16)
    xw_bf = (xs * expanded[CHUNK:]).astype(BF16)

    gw = D_INNER // SSM_GROUPS
    pair_w = 2 * SSM_HEAD_DIM
    lane = lax.broadcasted_iota(jnp.int32, (CHUNK, pair_w), 1)
    y_parts = []
    for g in range(SSM_GROUPS):
        b_g = bc[:, g * D_STATE:(g + 1) * D_STATE]
        c_bf = bc[:, BC_DIM // 2 + g * D_STATE:BC_DIM // 2 + (g + 1) * D_STATE].astype(BF16)
        cb = _dot_nt(c_bf, b_g.astype(BF16))
        s_g = state_ref[g]
        cols = slice(g * gw, (g + 1) * gw)
        y_off = _dot(c_bf, s_g.astype(BF16)) * e_acs_full[:, cols]
        for pr in range(HEADS_PER_GROUP // 2):
            h0 = g * HEADS_PER_GROUP + 2 * pr
            x_pair = xs_bf[:, h0 * SSM_HEAD_DIM:(h0 + 2) * SSM_HEAD_DIM]
            ys = []
            for h in (h0, h0 + 1):
                seg = a_cs[:, h:h + 1] - key_t[h:h + 1, :]
                m = (cb * jnp.exp(jnp.where(tril, seg, -jnp.inf))).astype(BF16)
                ys.append(_dot(m, x_pair))
            y_parts.append(jnp.where(lane < SSM_HEAD_DIM, ys[0], ys[1])
                           + y_off[:, pr * pair_w:(pr + 1) * pair_w])
        state_ref[g] = (s_g * e_acs_full[CHUNK - 1:CHUNK, cols]
                        + _dot(b_g.T.astype(BF16), xw_bf[:, cols]))

    y = jnp.concatenate(y_parts, axis=1) + xs * dskip_ref[...]
    y = y * _silu(z_raw.astype(F32))
    for g in range(SSM_GROUPS):
        yg = y[:, g * gw:(g + 1) * gw]
        var = jnp.mean(yg * yg, axis=-1, keepdims=True)
        y_ref[:, g * gw:(g + 1) * gw] = (
            yg * lax.rsqrt(var + EPS) * nw_ref[:, g * gw:(g + 1) * gw]).astype(y_ref.dtype)


def _ssd_constants():
    t = jnp.arange(CONV_SHIFT_ROWS)[:, None]
    j = jnp.arange(CHUNK + PREV_ROWS)[None, :]
    shift = (j == PREV_ROWS + (t % CHUNK) - (t // CHUNK + 1))
    shift = jnp.stack([shift, shift & (j >= PREV_ROWS)]).astype(BF16)
    r = jnp.arange(LANES)[:, None]
    c = jnp.arange(D_INNER)[None, :]
    expand = ((r < 3 * SSM_HEADS) & (r % SSM_HEADS == c // SSM_HEAD_DIM)).astype(BF16)
    return shift, expand


def _ssd(main, dt_raw, cwx, cbx, cwbc, cbbc, dtb, alog, dskip, nw, batch, seq):
    rows = min(SSD_ROWS, seq)
    nc = seq // rows
    tokens = batch * seq
    shift, expand = _ssd_constants()
    row_idx = lambda b, c: b * nc + c
    prev_idx = lambda b, c: jnp.maximum((b * nc + c) * (rows // PREV_ROWS) - 1, 0)
    const = lambda b, c: (0, 0)
    return pl.pallas_call(
        _ssd_kernel,
        grid=(batch, nc),
        in_specs=[
            pl.BlockSpec((rows, D_INNER), lambda b, c: (row_idx(b, c), 0)),
            pl.BlockSpec((rows, D_INNER), lambda b, c: (row_idx(b, c), 1)),
            pl.BlockSpec((rows, BC_DIM), lambda b, c: (row_idx(b, c), 4)),
            pl.BlockSpec((PREV_ROWS, D_INNER), lambda b, c: (prev_idx(b, c), 0)),
            pl.BlockSpec((PREV_ROWS, BC_DIM), lambda b, c: (prev_idx(b, c), 4)),
            pl.BlockSpec((rows, DT_PAD), lambda b, c: (row_idx(b, c), 0)),
            pl.BlockSpec((CONV_WIDTH, D_INNER), const),
            pl.BlockSpec((1, D_INNER), const),
            pl.BlockSpec((CONV_WIDTH, BC_DIM), const),
            pl.BlockSpec((1, BC_DIM), const),
            pl.BlockSpec((SSM_HEADS, 1), const),
            pl.BlockSpec((SSM_HEADS, 1), const),
            pl.BlockSpec((1, D_INNER), const),
            pl.BlockSpec((1, D_INNER), const),
            pl.BlockSpec((2, CONV_SHIFT_ROWS, CHUNK + PREV_ROWS), lambda b, c: (0, 0, 0)),
            pl.BlockSpec((LANES, D_INNER), const),
        ],
        out_specs=pl.BlockSpec((rows, D_INNER), lambda b, c: (row_idx(b, c), 0)),
        out_shape=jax.ShapeDtypeStruct((tokens, D_INNER), BF16),
        scratch_shapes=[
            pltpu.VMEM((SSM_GROUPS, D_STATE, D_INNER // SSM_GROUPS), F32),
        ],
        compiler_params=pltpu.CompilerParams(
            dimension_semantics=("parallel", "arbitrary"), vmem_limit_bytes=VMEM_LIMIT),
        name="ssd",
    )(main, main, main, main, main, dt_raw, cwx, cbx, cwbc, cbbc, dtb, alog, dskip, nw, shift, expand)


def _rotate_half_rows(x_t):
    half = ATTN_HEAD_DIM // 2
    parts = []
    for m in range(2):
        base = m * ATTN_HEAD_DIM
        parts += [x_t[base + half:base + 2 * half], x_t[base:base + half]]
    return jnp.concatenate(parts, axis=0)


def _rotate_half_lanes(x):
    half = ATTN_HEAD_DIM // 2
    lane = lax.broadcasted_iota(jnp.int32, x.shape, 1)
    return jnp.where((lane % ATTN_HEAD_DIM) < half,
                     pltpu.roll(x, LANES - half, 1), pltpu.roll(x, half, 1))


def _attn_kernel(q_ref, k_ref, v_ref, z_ref, cosq_ref, sinq_ref, cosk_ref, sink_ref,
                 lq1_ref, lk1_ref, lq2_ref, lk2_ref, sw_ref,
                 o_ref, vt_ref, kr_ref, *scratch, tq, tk, lambda_init):
    ones_rows = (lax.broadcasted_iota(jnp.int32, (VT_ROWS - ATTN_V_DIM, tk), 0) == 0).astype(BF16)
    for c in range(vt_ref.shape[0]):
        rows = slice(c * tk, (c + 1) * tk)
        vt_ref[c, :ATTN_V_DIM, :] = v_ref[rows, :].astype(F32).T.astype(BF16)
        vt_ref[c, ATTN_V_DIM:, :] = ones_rows
        k = k_ref[rows, :].astype(F32)
        kr_ref[rows, :] = (k * cosk_ref[rows, :]
                           + _rotate_half_lanes(k) * sink_ref[rows, :]).astype(BF16)

    def qblock(qi, carry):
        _attn_qblock(qi, q_ref, z_ref, cosq_ref, sinq_ref, lq1_ref, lk1_ref, lq2_ref, lk2_ref, sw_ref,
                     o_ref, vt_ref, kr_ref, *scratch, tq=tq, tk=tk, lambda_init=lambda_init)
        return carry

    lax.fori_loop(0, q_ref.shape[0] // tq, qblock, 0)


def _attn_qblock(qi, q_ref, z_ref, cosq_ref, sinq_ref, lq1_ref, lk1_ref, lq2_ref, lk2_ref, sw_ref,
                 o_ref, vt_ref, kr_ref, qq_ref, s0_ref, s1_ref, mb0_ref, mb1_ref, p0_ref, p1_ref,
                 a0_ref, a1_ref, m_ref, acc_ref, *, tq, tk, lambda_init):
    q_rows = pl.ds(pl.multiple_of(qi * tq, tq), tq)
    q_t = q_ref[q_rows, :].astype(F32).T
    q_t = q_t * cosq_ref[qi] + _rotate_half_rows(q_t) * sinq_ref[qi]
    row = lax.broadcasted_iota(jnp.int32, q_t.shape, 0)
    qq_ref[...] = jnp.concatenate([jnp.where(row < ATTN_HEAD_DIM, q_t, 0.0),
                                   jnp.where(row >= ATTN_HEAD_DIM, q_t, 0.0)], axis=1).astype(BF16)

    m_ref[...] = jnp.full_like(m_ref, -jnp.inf)
    acc_ref[...] = jnp.zeros_like(acc_ref)
    p1_ref[...] = jnp.zeros_like(p1_ref)
    a1_ref[...] = jnp.ones_like(a1_ref)

    def scores(j, s_ref, mb_ref):
        start = j * tk if isinstance(j, int) else pl.multiple_of(j * tk, tk)
        s = _dot(kr_ref[pl.ds(start, tk), :], qq_ref[...])
        s_ref[...] = s
        mb_ref[...] = jnp.max(s, axis=0, keepdims=True)

    def softmax(s_ref, mb_ref, p_ref, a_ref, diag_half=None):
        s = s_ref[...]
        if diag_half is None:
            mb = mb_ref[...]
        else:
            key = lax.broadcasted_iota(jnp.int32, s.shape, 0) + diag_half * tk
            qry = lax.broadcasted_iota(jnp.int32, s.shape, 1) & (tq - 1)
            s = jnp.where(key <= qry, s, -jnp.inf)
            mb = jnp.max(s, axis=0, keepdims=True)
        m_prev = m_ref[...]
        m_new = jnp.maximum(m_prev, mb)
        a_ref[...] = jnp.exp2(m_prev - m_new)
        p_ref[...] = jnp.exp2((s - m_new).astype(BF16))
        m_ref[...] = m_new

    def value(j, p_ref, a_ref):
        acc_ref[...] = acc_ref[...] * a_ref[...] + _dot(vt_ref[jnp.maximum(j, 0)], p_ref[...])

    scores(0, s0_ref, mb0_ref)

    def body(k, carry):
        softmax(s0_ref, mb0_ref, p0_ref, a0_ref)
        value(2 * k - 1, p1_ref, a1_ref)
        scores(2 * k + 1, s1_ref, mb1_ref)
        softmax(s1_ref, mb1_ref, p1_ref, a1_ref)
        value(2 * k, p0_ref, a0_ref)
        scores(2 * k + 2, s0_ref, mb0_ref)
        return carry

    lax.fori_loop(0, qi, body, 0)
    scores(2 * qi + 1, s1_ref, mb1_ref)
    softmax(s0_ref, mb0_ref, p0_ref, a0_ref, diag_half=0)
    value(2 * qi - 1, p1_ref, a1_ref)
    softmax(s1_ref, mb1_ref, p1_ref, a1_ref, diag_half=1)
    value(2 * qi, p0_ref, a0_ref)
    value(2 * qi + 1, p1_ref, a1_ref)

    lam = (jnp.exp(jnp.sum(lq1_ref[...] * lk1_ref[...], axis=-1, keepdims=True))
           - jnp.exp(jnp.sum(lq2_ref[...] * lk2_ref[...], axis=-1, keepdims=True))
           + lambda_init)
    o_maps = acc_ref[:ATTN_V_DIM, :] * (1.0 / acc_ref[ATTN_V_DIM:ATTN_V_DIM + 1, :])
    o = o_maps[:, :tq] - lam * o_maps[:, tq:]
    var = jnp.mean(o * o, axis=0, keepdims=True)
    o = o * lax.rsqrt(var + EPS) * sw_ref[...] * (1.0 - lambda_init)
    o_ref[q_rows, :] = (o.T * _silu(z_ref[q_rows, :].astype(F32))).astype(o_ref.dtype)


def _attention(main, rope, lq1, lk1, lq2, lk2, subln_w, batch, seq, lambda_init):
    tq = min(ATTN_BLOCK, seq)
    tk = tq // 2
    assert tq & (tq - 1) == 0 and seq % tq == 0
    nq = seq // tq
    tokens = batch * seq
    q_col = Q_TILE * COL_TILE // ATTN_V_DIM
    k_col = K_TILE * COL_TILE // ATTN_V_DIM
    v_col = (K_TILE + 1) * COL_TILE // ATTN_V_DIM
    z_col = (K_TILE + 2) * COL_TILE // ATTN_V_DIM
    const = lambda b, h: (0, 0)
    cos_q, sin_q, cos_k, sin_k = rope
    cos_q = cos_q.reshape(ATTN_V_DIM, nq, tq).transpose(1, 0, 2)
    sin_q = sin_q.reshape(ATTN_V_DIM, nq, tq).transpose(1, 0, 2)
    return pl.pallas_call(
        functools.partial(_attn_kernel, tq=tq, tk=tk, lambda_init=lambda_init),
        grid=(batch, ATTN_HEADS),
        in_specs=[
            pl.BlockSpec((seq, ATTN_V_DIM), lambda b, h: (b, q_col + h)),
            pl.BlockSpec((seq, ATTN_V_DIM), lambda b, h: (b, k_col + h)),
            pl.BlockSpec((seq, ATTN_V_DIM), lambda b, h: (b, v_col + h)),
            pl.BlockSpec((seq, ATTN_V_DIM), lambda b, h: (b, z_col + h)),
            pl.BlockSpec((nq, ATTN_V_DIM, tq), lambda b, h: (0, 0, 0)),
            pl.BlockSpec((nq, ATTN_V_DIM, tq), lambda b, h: (0, 0, 0)),
            pl.BlockSpec((seq, ATTN_V_DIM), const),
            pl.BlockSpec((seq, ATTN_V_DIM), const),
            pl.BlockSpec((1, ATTN_HEAD_DIM), const),
            pl.BlockSpec((1, ATTN_HEAD_DIM), const),
            pl.BlockSpec((1, ATTN_HEAD_DIM), const),
            pl.BlockSpec((1, ATTN_HEAD_DIM), const),
            pl.BlockSpec((ATTN_V_DIM, 1), const),
        ],
        out_specs=pl.BlockSpec((seq, ATTN_V_DIM), lambda b, h: (b, h)),
        out_shape=jax.ShapeDtypeStruct((tokens, ATTN_WIDTH), BF16),
        scratch_shapes=[
            pltpu.VMEM((seq // tk, VT_ROWS, tk), BF16),
            pltpu.VMEM((seq, ATTN_V_DIM), BF16),
            pltpu.VMEM((ATTN_V_DIM, 2 * tq), BF16),
            pltpu.VMEM((tk, 2 * tq), F32),
            pltpu.VMEM((tk, 2 * tq), F32),
            pltpu.VMEM((1, 2 * tq), F32),
            pltpu.VMEM((1, 2 * tq), F32),
            pltpu.VMEM((tk, 2 * tq), BF16),
            pltpu.VMEM((tk, 2 * tq), BF16),
            pltpu.VMEM((1, 2 * tq), F32),
            pltpu.VMEM((1, 2 * tq), F32),
            pltpu.VMEM((1, 2 * tq), F32),
            pltpu.VMEM((VT_ROWS, 2 * tq), F32),
        ],
        compiler_params=pltpu.CompilerParams(
            dimension_semantics=("parallel", "parallel"),
            vmem_limit_bytes=VMEM_LIMIT),
        name="diff_attn",
    )(main, main, main, main, cos_q, sin_q, cos_k, sin_k, lq1, lk1, lq2, lk2, subln_w)


def _merge_kernel(x_ref, ys_ref, ya_ref, gs_ref, ga_ref, gb_ref, ps_ref, pa_ref, wo_ref, fw_ref,
                  o_ref, *, final_norm):
    proj_s = _dot(ys_ref[...], ps_ref[...])
    proj_a = _dot(ya_ref[...], pa_ref[...])
    g_s = 1.0 / (1.0 + jnp.exp(-(gs_ref[...].astype(F32) + gb_ref[:, :D_MODEL])))
    g_a = 1.0 / (1.0 + jnp.exp(-(ga_ref[...].astype(F32) + gb_ref[:, D_MODEL:])))
    merged = g_s * proj_s + g_a * proj_a
    x = x_ref[...] + _dot(merged.astype(BF16), wo_ref[...])
    if final_norm:
        var = jnp.mean(x * x, axis=-1, keepdims=True)
        x = x * lax.rsqrt(var + EPS) * fw_ref[...]
    o_ref[...] = x


def _merge(x2, y_ssm, y_attn, main, gate_b, p_ssm, p_attn, w_out, final_w, final_norm):
    tokens = x2.shape[0]
    tm = min(512, tokens)
    const = lambda i: (0, 0)
    return pl.pallas_call(
        functools.partial(_merge_kernel, final_norm=final_norm),
        grid=(tokens // tm,),
        in_specs=[
            pl.BlockSpec((tm, D_MODEL), lambda i: (i, 0)),
            pl.BlockSpec((tm, D_INNER), lambda i: (i, 0)),
            pl.BlockSpec((tm, ATTN_WIDTH), lambda i: (i, 0)),
            pl.BlockSpec((tm, D_MODEL), lambda i: (i, N_COL_TILES - 2)),
            pl.BlockSpec((tm, D_MODEL), lambda i: (i, N_COL_TILES - 1)),
            pl.BlockSpec((1, 2 * D_MODEL), const),
            pl.BlockSpec((D_INNER, D_MODEL), const),
            pl.BlockSpec((ATTN_WIDTH, D_MODEL), const),
            pl.BlockSpec((D_MODEL, D_MODEL), const),
            pl.BlockSpec((1, D_MODEL), const),
        ],
        out_specs=pl.BlockSpec((tm, D_MODEL), lambda i: (i, 0)),
        out_shape=jax.ShapeDtypeStruct((tokens, D_MODEL), F32),
        compiler_params=pltpu.CompilerParams(
            dimension_semantics=("parallel",), vmem_limit_bytes=VMEM_LIMIT),
        name="merge",
    )(x2, y_ssm, y_attn, main, main, gate_b, p_ssm, p_attn, w_out, final_w)


def _rope_tables(seq):
    inv = 1.0 / (ROPE_THETA ** (jnp.arange(0, ATTN_HEAD_DIM, 2, dtype=F32) / ATTN_HEAD_DIM))
    f = jnp.arange(seq, dtype=F32)[:, None] * inv[None, :]
    cos = jnp.cos(f)
    sin = jnp.sin(f)
    cos_k = jnp.concatenate([cos, cos, cos, cos], axis=-1)
    sin_k = jnp.concatenate([-sin, sin, -sin, sin], axis=-1)
    q_scale = ATTN_HEAD_DIM ** -0.5 * LOG2_E
    return (cos_k.T * q_scale, sin_k.T * q_scale, cos_k, sin_k)


def kernel(x, norm_w, w_in, conv_w, conv_b, dt_bias, a_log, d_skip, ssm_norm_w,
           lambda_q1, lambda_k1, lambda_q2, lambda_k2, subln_w, gate_b,
           w_proj_ssm, w_proj_attn, w_out, final_norm_w):
    batch, seq, _ = x.shape
    depth = norm_w.shape[0]
    assert seq % CHUNK == 0
    rope = _rope_tables(seq)
    x2 = x.reshape(batch * seq, D_MODEL)
    head_col = lambda v: v.astype(F32).reshape(SSM_HEADS, 1)

    dt_off = D_INNER + XBC_DIM
    for l in range(depth):
        lambda_init = 0.8 - 0.6 * math.exp(-0.3 * l)
        w = w_in[l]
        w_main = jnp.concatenate(
            [w[:, D_INNER:2 * D_INNER], w[:, :D_INNER], w[:, 2 * D_INNER:dt_off],
             w[:, dt_off + SSM_HEADS:]], axis=1).astype(BF16)
        w_dt = jnp.pad(w[:, dt_off:dt_off + SSM_HEADS], ((0, 0), (0, DT_PAD - SSM_HEADS)))
        wdt_hi = w_dt.astype(BF16)
        wdt_lo = (w_dt - wdt_hi.astype(F32)).astype(BF16)

        main, dt_raw = _inproj(x2, norm_w[l].reshape(1, D_MODEL), w_main, wdt_hi, wdt_lo, seq)

        y_ssm = _ssd(
            main, dt_raw,
            conv_w[l][:, :D_INNER], conv_b[l][:D_INNER].reshape(1, D_INNER),
            conv_w[l][:, D_INNER:], conv_b[l][D_INNER:].reshape(1, BC_DIM),
            head_col(dt_bias[l]), head_col(a_log[l]),
            jnp.repeat(d_skip[l].astype(F32), SSM_HEAD_DIM).reshape(1, D_INNER),
            ssm_norm_w[l].reshape(1, D_INNER), batch, seq)

        y_attn = _attention(
            main, rope, lambda_q1[l].reshape(1, -1), lambda_k1[l].reshape(1, -1),
            lambda_q2[l].reshape(1, -1), lambda_k2[l].reshape(1, -1),
            subln_w[l].reshape(ATTN_V_DIM, 1), batch, seq, lambda_init)

        x2 = _merge(x2, y_ssm, y_attn, main, gate_b[l].reshape(1, 2 * D_MODEL),
                    w_proj_ssm[l].astype(BF16), w_proj_attn[l].astype(BF16),
                    w_out[l].astype(BF16), final_norm_w.reshape(1, D_MODEL),
                    final_norm=(l == depth - 1))
    return x2.reshape(batch, seq, D_MODEL)
```

```python
import functools
import math

import jax
import jax.numpy as jnp
from jax import lax
from jax.experimental import pallas as pl
from jax.experimental.pallas import tpu as pltpu

F32 = jnp.float32
BF16 = jnp.bfloat16

D_MODEL = 1024
EPS = 1e-5
D_INNER = 2 * D_MODEL
SSM_HEAD_DIM = 64
SSM_HEADS = D_INNER // SSM_HEAD_DIM
SSM_GROUPS = 4
HEADS_PER_GROUP = SSM_HEADS // SSM_GROUPS
D_STATE = 128
CONV_WIDTH = 4
CHUNK = 128
BC_DIM = 2 * SSM_GROUPS * D_STATE
XBC_DIM = D_INNER + BC_DIM
ATTN_HEADS = 8
ATTN_HEAD_DIM = 64
ATTN_V_DIM = 2 * ATTN_HEAD_DIM
ATTN_WIDTH = ATTN_HEADS * ATTN_V_DIM
ROPE_THETA = 10000.0
LOG2_E = math.log2(math.e)
ATTN_BLOCK = 512
VT_ROWS = ATTN_V_DIM + 16

LANES = 128
PREV_ROWS = 16
CONV_SHIFT_ROWS = (CONV_WIDTH - 1) * CHUNK

COL_TILE = 1024
N_COL_TILES = 11
MAIN_DIM = COL_TILE * N_COL_TILES
Q_TILE = 5
K_TILE = 6
DT_PAD = LANES

INPROJ_ROWS = 2048
SSD_ROWS = 4 * CHUNK
VMEM_LIMIT = 56 * 1024 * 1024


def _silu(x):
    h = 0.5 * x
    return h + h * jnp.tanh(h)


def _dot(a, b):
    return jnp.dot(a, b, preferred_element_type=F32)


def _dot_nt(a, b):
    return lax.dot_general(a, b, (((1,), (1,)), ((), ())), preferred_element_type=F32)


def _inproj_kernel(x_ref, nw_ref, w_ref, wdt_hi_ref, wdt_lo_ref, out_ref, dt_ref, h_ref):
    @pl.when(pl.program_id(1) == 0)
    def _():
        x = x_ref[...]
        var = jnp.mean(x * x, axis=-1, keepdims=True)
        h = x * lax.rsqrt(var + EPS) * nw_ref[...]
        h_hi = h.astype(BF16)
        h_ref[...] = h_hi
        h_lo = (h - h_hi.astype(F32)).astype(BF16)
        dt_ref[...] = (_dot(h_hi, wdt_hi_ref[...]) + _dot(h_lo, wdt_hi_ref[...])
                       + _dot(h_hi, wdt_lo_ref[...]))

    out_ref[...] = _dot(h_ref[...], w_ref[...]).astype(out_ref.dtype)


def _inproj(x2, norm_w, w_main, wdt_hi, wdt_lo, seq):
    tokens = x2.shape[0]
    tm = min(INPROJ_ROWS, seq)
    return pl.pallas_call(
        _inproj_kernel,
        grid=(tokens // tm, N_COL_TILES),
        in_specs=[
            pl.BlockSpec((tm, D_MODEL), lambda i, j: (i, 0)),
            pl.BlockSpec((1, D_MODEL), lambda i, j: (0, 0)),
            pl.BlockSpec((D_MODEL, COL_TILE), lambda i, j: (0, j)),
            pl.BlockSpec((D_MODEL, DT_PAD), lambda i, j: (0, 0)),
            pl.BlockSpec((D_MODEL, DT_PAD), lambda i, j: (0, 0)),
        ],
        out_specs=[
            pl.BlockSpec((tm, COL_TILE), lambda i, j: (i, j)),
            pl.BlockSpec((tm, DT_PAD), lambda i, j: (i, 0)),
        ],
        out_shape=[
            jax.ShapeDtypeStruct((tokens, MAIN_DIM), BF16),
            jax.ShapeDtypeStruct((tokens, DT_PAD), F32),
        ],
        scratch_shapes=[pltpu.VMEM((tm, D_MODEL), BF16)],
        compiler_params=pltpu.CompilerParams(
            dimension_semantics=("parallel", "arbitrary"), vmem_limit_bytes=VMEM_LIMIT),
        name="inproj",
    )(x2, norm_w, w_main, wdt_hi, wdt_lo)


def _split3(v):
    hi = v.astype(BF16)
    r = v - hi.astype(F32)
    mid = r.astype(BF16)
    lo = (r - mid.astype(F32)).astype(BF16)
    return hi, mid, lo


def _conv_silu(u, prev, shift, w_ref, b_ref):
    shifted = _dot(shift, jnp.concatenate([prev, u], axis=0))
    out = b_ref[...] + w_ref[CONV_WIDTH - 1:CONV_WIDTH, :] * u.astype(F32)
    for k in range(1, CONV_WIDTH):
        out = out + w_ref[CONV_WIDTH - 1 - k:CONV_WIDTH - k, :] * shifted[(k - 1) * CHUNK:k * CHUNK, :]
    return _silu(out)


def _pad_rows_t(v_t):
    pad = jnp.zeros((LANES - v_t.shape[0], v_t.shape[1]), F32)
    return jnp.concatenate([v_t, pad], axis=0).T


def _split3_t(v_t):
    hi, mid, lo = _split3(v_t)
    return _pad_rows_t(jnp.concatenate(
        [hi.astype(F32), mid.astype(F32), lo.astype(F32)], axis=0)).astype(BF16)


def _ssd_kernel(xs_ref, z_ref, bc_ref, xs_prev_ref, bc_prev_ref, dt_ref, cwx_ref, cbx_ref,
                cwbc_ref, cbbc_ref, dtb_ref, alog_ref, dskip_ref, nw_ref, shift_ref, expand_ref,
                y_ref, state_ref):
    first = pl.program_id(1) == 0

    @pl.when(first)
    def _():
        state_ref[...] = jnp.zeros_like(state_ref)

    for sub in range(xs_ref.shape[0] // CHUNK):
        rows = slice(sub * CHUNK, (sub + 1) * CHUNK)
        if sub == 0:
            shift = shift_ref[jnp.where(first, 1, 0)]
            xs_prev, bc_prev = xs_prev_ref[...], bc_prev_ref[...]
        else:
            shift = shift_ref[0]
            hist = slice(sub * CHUNK - PREV_ROWS, sub * CHUNK)
            xs_prev, bc_prev = xs_ref[hist, :], bc_ref[hist, :]
        _ssd_chunk(xs_ref[rows, :], z_ref[rows, :], bc_ref[rows, :], xs_prev, bc_prev, dt_ref[rows, :],
                   shift, cwx_ref, cbx_ref, cwbc_ref, cbbc_ref, dtb_ref, alog_ref, dskip_ref, nw_ref,
                   expand_ref, y_ref.at[rows, :], state_ref)


def _ssd_chunk(xs_raw, z_raw, bc_raw, xs_prev, bc_prev, dt_raw, shift, cwx_ref, cbx_ref, cwbc_ref, cbbc_ref,
               dtb_ref, alog_ref, dskip_ref, nw_ref, expand_ref, y_ref, state_ref):
    xs = _conv_silu(xs_raw, xs_prev, shift, cwx_ref, cbx_ref)
    bc = _conv_silu(bc_raw, bc_prev, shift, cwbc_ref, cbbc_ref)

    dt_in = dt_raw.T[:SSM_HEADS] + dtb_ref[...]
    dt = jnp.maximum(dt_in, 0.0) + jnp.log1p(jnp.exp(-jnp.abs(dt_in)))
    adt = dt * (-jnp.exp(alog_ref[...]))

    row = lax.broadcasted_iota(jnp.int32, (CHUNK, CHUNK), 0)
    col = lax.broadcasted_iota(jnp.int32, (CHUNK, CHUNK), 1)
    tril = row >= col
    triu_bf = (row <= col).astype(BF16)
    hi, mid, lo = _split3(adt)
    a_cs_t = _dot(hi, triu_bf) + _dot(mid, triu_bf) + _dot(lo, triu_bf)
    key_t = a_cs_t - jnp.log(dt)
    a_cs = _pad_rows_t(a_cs_t)
    e_acs_t = jnp.exp(a_cs_t)
    w_end_t = dt * jnp.exp(a_cs_t[:, CHUNK - 1:CHUNK] - a_cs_t)
    expanded = _dot(jnp.concatenate([_split3_t(e_acs_t), _split3_t(w_end_t)], axis=0),
                    expand_ref[...])
    e_acs_full = expanded[:CHUNK]
    xs_bf = xs.astype(BF16)
    xw_bf = (xs * expanded[CHUNK:]).astype(BF16)

    gw = D_INNER // SSM_GROUPS
    pair_w = 2 * SSM_HEAD_DIM
    lane = lax.broadcasted_iota(jnp.int32, (CHUNK, pair_w), 1)
    y_parts = []
    for g in range(SSM_GROUPS):
        b_g = bc[:, g * D_STATE:(g + 1) * D_STATE]
        c_bf = bc[:, BC_DIM // 2 + g * D_STATE:BC_DIM // 2 + (g + 1) * D_STATE].astype(BF16)
        cb = _dot_nt(c_bf, b_g.astype(BF16))
        s_g = state_ref[g]
        cols = slice(g * gw, (g + 1) * gw)
        y_off = _dot(c_bf, s_g.astype(BF16)) * e_acs_full[:, cols]
        for pr in range(HEADS_PER_GROUP // 2):
            h0 = g * HEADS_PER_GROUP + 2 * pr
            x_pair = xs_bf[:, h0 * SSM_HEAD_DIM:(h0 + 2) * SSM_HEAD_DIM]
            ys = []
            for h in (h0, h0 + 1):
                seg = a_cs[:, h:h + 1] - key_t[h:h + 1, :]
                m = (cb * jnp.exp(jnp.where(tril, seg, -jnp.inf))).astype(BF16)
                ys.append(_dot(m, x_pair))
            y_parts.append(jnp.where(lane < SSM_HEAD_DIM, ys[0], ys[1])
                           + y_off[:, pr * pair_w:(pr + 1) * pair_w])
        state_ref[g] = (s_g * e_acs_full[CHUNK - 1:CHUNK, cols]
                        + _dot(b_g.T.astype(BF16), xw_bf[:, cols]))

    y = jnp.concatenate(y_parts, axis=1) + xs * dskip_ref[...]
    y = y * _silu(z_raw.astype(F32))
    for g in range(SSM_GROUPS):
        yg = y[:, g * gw:(g + 1) * gw]
        var = jnp.mean(yg * yg, axis=-1, keepdims=True)
        y_ref[:, g * gw:(g + 1) * gw] = (
            yg * lax.rsqrt(var + EPS) * nw_ref[:, g * gw:(g + 1) * gw]).astype(y_ref.dtype)


def _ssd_constants():
    t = jnp.arange(CONV_SHIFT_ROWS)[:, None]
    j = jnp.arange(CHUNK + PREV_ROWS)[None, :]
    shift = (j == PREV_ROWS + (t % CHUNK) - (t // CHUNK + 1))
    shift = jnp.stack([shift, shift & (j >= PREV_ROWS)]).astype(BF16)
    r = jnp.arange(LANES)[:, None]
    c = jnp.arange(D_INNER)[None, :]
    expand = ((r < 3 * SSM_HEADS) & (r % SSM_HEADS == c // SSM_HEAD_DIM)).astype(BF16)
    return shift, expand


def _ssd(main, dt_raw, cwx, cbx, cwbc, cbbc, dtb, alog, dskip, nw, batch, seq):
    rows = min(SSD_ROWS, seq)
    nc = seq // rows
    tokens = batch * seq
    shift, expand = _ssd_constants()
    row_idx = lambda b, c: b * nc + c
    prev_idx = lambda b, c: jnp.maximum((b * nc + c) * (rows // PREV_ROWS) - 1, 0)
    const = lambda b, c: (0, 0)
    return pl.pallas_call(
        _ssd_kernel,
        grid=(batch, nc),
        in_specs=[
            pl.BlockSpec((rows, D_INNER), lambda b, c: (row_idx(b, c), 0)),
            pl.BlockSpec((rows, D_INNER), lambda b, c: (row_idx(b, c), 1)),
            pl.BlockSpec((rows, BC_DIM), lambda b, c: (row_idx(b, c), 4)),
            pl.BlockSpec((PREV_ROWS, D_INNER), lambda b, c: (prev_idx(b, c), 0)),
            pl.BlockSpec((PREV_ROWS, BC_DIM), lambda b, c: (prev_idx(b, c), 4)),
            pl.BlockSpec((rows, DT_PAD), lambda b, c: (row_idx(b, c), 0)),
            pl.BlockSpec((CONV_WIDTH, D_INNER), const),
            pl.BlockSpec((1, D_INNER), const),
            pl.BlockSpec((CONV_WIDTH, BC_DIM), const),
            pl.BlockSpec((1, BC_DIM), const),
            pl.BlockSpec((SSM_HEADS, 1), const),
            pl.BlockSpec((SSM_HEADS, 1), const),
            pl.BlockSpec((1, D_INNER), const),
            pl.BlockSpec((1, D_INNER), const),
            pl.BlockSpec((2, CONV_SHIFT_ROWS, CHUNK + PREV_ROWS), lambda b, c: (0, 0, 0)),
            pl.BlockSpec((LANES, D_INNER), const),
        ],
        out_specs=pl.BlockSpec((rows, D_INNER), lambda b, c: (row_idx(b, c), 0)),
        out_shape=jax.ShapeDtypeStruct((tokens, D_INNER), BF16),
        scratch_shapes=[
            pltpu.VMEM((SSM_GROUPS, D_STATE, D_INNER // SSM_GROUPS), F32),
        ],
        compiler_params=pltpu.CompilerParams(
            dimension_semantics=("parallel", "arbitrary"), vmem_limit_bytes=VMEM_LIMIT),
        name="ssd",
    )(main, main, main, main, main, dt_raw, cwx, cbx, cwbc, cbbc, dtb, alog, dskip, nw, shift, expand)


def _rotate_half_rows(x_t):
    half = ATTN_HEAD_DIM // 2
    parts = []
    for m in range(2):
        base = m * ATTN_HEAD_DIM
        parts += [x_t[base + half:base + 2 * half], x_t[base:base + half]]
    return jnp.concatenate(parts, axis=0)


def _rotate_half_lanes(x):
    half = ATTN_HEAD_DIM // 2
    lane = lax.broadcasted_iota(jnp.int32, x.shape, 1)
    return jnp.where((lane % ATTN_HEAD_DIM) < half,
                     pltpu.roll(x, LANES - half, 1), pltpu.roll(x, half, 1))


def _attn_kernel(q_ref, k_ref, v_ref, z_ref, cosq_ref, sinq_ref, cosk_ref, sink_ref,
                 lq1_ref, lk1_ref, lq2_ref, lk2_ref, sw_ref,
                 o_ref, vt_ref, kr_ref, *scratch, tq, tk, lambda_init):
    ones_rows = (lax.broadcasted_iota(jnp.int32, (VT_ROWS - ATTN_V_DIM, tk), 0) == 0).astype(BF16)
    for c in range(vt_ref.shape[0]):
        rows = slice(c * tk, (c + 1) * tk)
        vt_ref[c, :ATTN_V_DIM, :] = v_ref[rows, :].astype(F32).T.astype(BF16)
        vt_ref[c, ATTN_V_DIM:, :] = ones_rows
        k = k_ref[rows, :].astype(F32)
        kr_ref[rows, :] = (k * cosk_ref[rows, :]
                           + _rotate_half_lanes(k) * sink_ref[rows, :]).astype(BF16)

    def qblock(qi, carry):
        _attn_qblock(qi, q_ref, z_ref, cosq_ref, sinq_ref, lq1_ref, lk1_ref, lq2_ref, lk2_ref, sw_ref,
                     o_ref, vt_ref, kr_ref, *scratch, tq=tq, tk=tk, lambda_init=lambda_init)
        return carry

    lax.fori_loop(0, q_ref.shape[0] // tq, qblock, 0)


def _attn_qblock(qi, q_ref, z_ref, cosq_ref, sinq_ref, lq1_ref, lk1_ref, lq2_ref, lk2_ref, sw_ref,
                 o_ref, vt_ref, kr_ref, qq_ref, s0_ref, s1_ref, mb0_ref, mb1_ref, p0_ref, p1_ref,
                 a0_ref, a1_ref, m_ref, acc_ref, *, tq, tk, lambda_init):
    q_rows = pl.ds(pl.multiple_of(qi * tq, tq), tq)
    q_t = q_ref[q_rows, :].astype(F32).T
    q_t = q_t * cosq_ref[qi] + _rotate_half_rows(q_t) * sinq_ref[qi]
    row = lax.broadcasted_iota(jnp.int32, q_t.shape, 0)
    qq_ref[...] = jnp.concatenate([jnp.where(row < ATTN_HEAD_DIM, q_t, 0.0),
                                   jnp.where(row >= ATTN_HEAD_DIM, q_t, 0.0)], axis=1).astype(BF16)

    m_ref[...] = jnp.full_like(m_ref, -jnp.inf)
    acc_ref[...] = jnp.zeros_like(acc_ref)
    p1_ref[...] = jnp.zeros_like(p1_ref)
    a1_ref[...] = jnp.ones_like(a1_ref)

    def scores(j, s_ref, mb_ref):
        start = j * tk if isinstance(j, int) else pl.multiple_of(j * tk, tk)
        s = _dot(kr_ref[pl.ds(start, tk), :], qq_ref[...])
        s_ref[...] = s
        mb_ref[...] = jnp.max(s, axis=0, keepdims=True)

    def softmax(s_ref, mb_ref, p_ref, a_ref, diag_half=None):
        s = s_ref[...]
        if diag_half is None:
            mb = mb_ref[...]
        else:
            key = lax.broadcasted_iota(jnp.int32, s.shape, 0) + diag_half * tk
            qry = lax.broadcasted_iota(jnp.int32, s.shape, 1) & (tq - 1)
            s = jnp.where(key <= qry, s, -jnp.inf)
            mb = jnp.max(s, axis=0, keepdims=True)
        m_prev = m_ref[...]
        m_new = jnp.maximum(m_prev, mb)
        a_ref[...] = jnp.exp2(m_prev - m_new)
        p_ref[...] = jnp.exp2((s - m_new).astype(BF16))
        m_ref[...] = m_new

    def value(j, p_ref, a_ref):
        acc_ref[...] = acc_ref[...] * a_ref[...] + _dot(vt_ref[jnp.maximum(j, 0)], p_ref[...])

    scores(0, s0_ref, mb0_ref)

    def body(k, carry):
        softmax(s0_ref, mb0_ref, p0_ref, a0_ref)
        value(2 * k - 1, p1_ref, a1_ref)
        scores(2 * k + 1, s1_ref, mb1_ref)
        softmax(s1_ref, mb1_ref, p1_ref, a1_ref)
        value(2 * k, p0_ref, a0_ref)
        scores(2 * k + 2, s0_ref, mb0_ref)
        return carry

    lax.fori_loop(0, qi, body, 0)
    scores(2 * qi + 1, s1_ref, mb1_ref)
    softmax(s0_ref, mb0_ref, p0_ref, a0_ref, diag_half=0)
    value(2 * qi - 1, p1_ref, a1_ref)
    softmax(s1_ref, mb1_ref, p1_ref, a1_ref, diag_half=1)
    value(2 * qi, p0_ref, a0_ref)
    value(2 * qi + 1, p1_ref, a1_ref)

    lam = (jnp.exp(jnp.sum(lq1_ref[...] * lk1_ref[...], axis=-1, keepdims=True))
           - jnp.exp(jnp.sum(lq2_ref[...] * lk2_ref[...], axis=-1, keepdims=True))
           + lambda_init)
    o_maps = acc_ref[:ATTN_V_DIM, :] * (1.0 / acc_ref[ATTN_V_DIM:ATTN_V_DIM + 1, :])
    o = o_maps[:, :tq] - lam * o_maps[:, tq:]
    var = jnp.mean(o * o, axis=0, keepdims=True)
    o = o * lax.rsqrt(var + EPS) * sw_ref[...] * (1.0 - lambda_init)
    o_ref[q_rows, :] = (o.T * _silu(z_ref[q_rows, :].astype(F32))).astype(o_ref.dtype)


def _attention(main, rope, lq1, lk1, lq2, lk2, subln_w, batch, seq, lambda_init):
    tq = min(ATTN_BLOCK, seq)
    tk = tq // 2
    assert tq & (tq - 1) == 0 and seq % tq == 0
    nq = seq // tq
    tokens = batch * seq
    q_col = Q_TILE * COL_TILE // ATTN_V_DIM
    k_col = K_TILE * COL_TILE // ATTN_V_DIM
    v_col = (K_TILE + 1) * COL_TILE // ATTN_V_DIM
    z_col = (K_TILE + 2) * COL_TILE // ATTN_V_DIM
    const = lambda b, h: (0, 0)
    cos_q, sin_q, cos_k, sin_k = rope
    cos_q = cos_q.reshape(ATTN_V_DIM, nq, tq).transpose(1, 0, 2)
    sin_q = sin_q.reshape(ATTN_V_DIM, nq, tq).transpose(1, 0, 2)
    return pl.pallas_call(
        functools.partial(_attn_kernel, tq=tq, tk=tk, lambda_init=lambda_init),
        grid=(batch, ATTN_HEADS),
        in_specs=[
            pl.BlockSpec((seq, ATTN_V_DIM), lambda b, h: (b, q_col + h)),
            pl.BlockSpec((seq, ATTN_V_DIM), lambda b, h: (b, k_col + h)),
            pl.BlockSpec((seq, ATTN_V_DIM), lambda b, h: (b, v_col + h)),
            pl.BlockSpec((seq, ATTN_V_DIM), lambda b, h: (b, z_col + h)),
            pl.BlockSpec((nq, ATTN_V_DIM, tq), lambda b, h: (0, 0, 0)),
            pl.BlockSpec((nq, ATTN_V_DIM, tq), lambda b, h: (0, 0, 0)),
            pl.BlockSpec((seq, ATTN_V_DIM), const),
            pl.BlockSpec((seq, ATTN_V_DIM), const),
            pl.BlockSpec((1, ATTN_HEAD_DIM), const),
            pl.BlockSpec((1, ATTN_HEAD_DIM), const),
            pl.BlockSpec((1, ATTN_HEAD_DIM), const),
            pl.BlockSpec((1, ATTN_HEAD_DIM), const),
            pl.BlockSpec((ATTN_V_DIM, 1), const),
        ],
        out_specs=pl.BlockSpec((seq, ATTN_V_DIM), lambda b, h: (b, h)),
        out_shape=jax.ShapeDtypeStruct((tokens, ATTN_WIDTH), BF16),
        scratch_shapes=[
            pltpu.VMEM((seq // tk, VT_ROWS, tk), BF16),
            pltpu.VMEM((seq, ATTN_V_DIM), BF16),
            pltpu.VMEM((ATTN_V_DIM, 2 * tq), BF16),
            pltpu.VMEM((tk, 2 * tq), F32),
            pltpu.VMEM((tk, 2 * tq), F32),
            pltpu.VMEM((1, 2 * tq), F32),
            pltpu.VMEM((1, 2 * tq), F32),
            pltpu.VMEM((tk, 2 * tq), BF16),
            pltpu.VMEM((tk, 2 * tq), BF16),
            pltpu.VMEM((1, 2 * tq), F32),
            pltpu.VMEM((1, 2 * tq), F32),
            pltpu.VMEM((1, 2 * tq), F32),
            pltpu.VMEM((VT_ROWS, 2 * tq), F32),
        ],
        compiler_params=pltpu.CompilerParams(
            dimension_semantics=("parallel", "parallel"),
            vmem_limit_bytes=VMEM_LIMIT),
        name="diff_attn",
    )(main, main, main, main, cos_q, sin_q, cos_k, sin_k, lq1, lk1, lq2, lk2, subln_w)


def _merge_kernel(x_ref, ys_ref, ya_ref, gs_ref, ga_ref, gb_ref, ps_ref, pa_ref, wo_ref, fw_ref,
                  o_ref, *, final_norm):
    proj_s = _dot(ys_ref[...], ps_ref[...])
    proj_a = _dot(ya_ref[...], pa_ref[...])
    g_s = 1.0 / (1.0 + jnp.exp(-(gs_ref[...].astype(F32) + gb_ref[:, :D_MODEL])))
    g_a = 1.0 / (1.0 + jnp.exp(-(ga_ref[...].astype(F32) + gb_ref[:, D_MODEL:])))
    merged = g_s * proj_s + g_a * proj_a
    x = x_ref[...] + _dot(merged.astype(BF16), wo_ref[...])
    if final_norm:
        var = jnp.mean(x * x, axis=-1, keepdims=True)
        x = x * lax.rsqrt(var + EPS) * fw_ref[...]
    o_ref[...] = x


def _merge(x2, y_ssm, y_attn, main, gate_b, p_ssm, p_attn, w_out, final_w, final_norm):
    tokens = x2.shape[0]
    tm = min(512, tokens)
    const = lambda i: (0, 0)
    return pl.pallas_call(
        functools.partial(_merge_kernel, final_norm=final_norm),
        grid=(tokens // tm,),
        in_specs=[
            pl.BlockSpec((tm, D_MODEL), lambda i: (i, 0)),
            pl.BlockSpec((tm, D_INNER), lambda i: (i, 0)),
            pl.BlockSpec((tm, ATTN_WIDTH), lambda i: (i, 0)),
            pl.BlockSpec((tm, D_MODEL), lambda i: (i, N_COL_TILES - 2)),
            pl.BlockSpec((tm, D_MODEL), lambda i: (i, N_COL_TILES - 1)),
            pl.BlockSpec((1, 2 * D_MODEL), const),
            pl.BlockSpec((D_INNER, D_MODEL), const),
            pl.BlockSpec((ATTN_WIDTH, D_MODEL), const),
            pl.BlockSpec((D_MODEL, D_MODEL), const),
            pl.BlockSpec((1, D_MODEL), const),
        ],
        out_specs=pl.BlockSpec((tm, D_MODEL), lambda i: (i, 0)),
        out_shape=jax.ShapeDtypeStruct((tokens, D_MODEL), F32),
        compiler_params=pltpu.CompilerParams(
            dimension_semantics=("parallel",), vmem_limit_bytes=VMEM_LIMIT),
        name="merge",
    )(x2, y_ssm, y_attn, main, main, gate_b, p_ssm, p_attn, w_out, final_w)


def _rope_tables(seq):
    inv = 1.0 / (ROPE_THETA ** (jnp.arange(0, ATTN_HEAD_DIM, 2, dtype=F32) / ATTN_HEAD_DIM))
    f = jnp.arange(seq, dtype=F32)[:, None] * inv[None, :]
    cos = jnp.cos(f)
    sin = jnp.sin(f)
    cos_k = jnp.concatenate([cos, cos, cos, cos], axis=-1)
    sin_k = jnp.concatenate([-sin, sin, -sin, sin], axis=-1)
    q_scale = ATTN_HEAD_DIM ** -0.5 * LOG2_E
    return (cos_k.T * q_scale, sin_k.T * q_scale, cos_k, sin_k)


def kernel(x, norm_w, w_in, conv_w, conv_b, dt_bias, a_log, d_skip, ssm_norm_w,
           lambda_q1, lambda_k1, lambda_q2, lambda_k2, subln_w, gate_b,
           w_proj_ssm, w_proj_attn, w_out, final_norm_w):
    batch, seq, _ = x.shape
    depth = norm_w.shape[0]
    assert seq % CHUNK == 0
    rope = _rope_tables(seq)
    x2 = x.reshape(batch * seq, D_MODEL)
    head_col = lambda v: v.astype(F32).reshape(SSM_HEADS, 1)

    dt_off = D_INNER + XBC_DIM
    for l in range(depth):
        lambda_init = 0.8 - 0.6 * math.exp(-0.3 * l)
        w = w_in[l]
        w_main = jnp.concatenate(
            [w[:, D_INNER:2 * D_INNER], w[:, :D_INNER], w[:, 2 * D_INNER:dt_off],
             w[:, dt_off + SSM_HEADS:]], axis=1).astype(BF16)
        w_dt = jnp.pad(w[:, dt_off:dt_off + SSM_HEADS], ((0, 0), (0, DT_PAD - SSM_HEADS)))
        wdt_hi = w_dt.astype(BF16)
        wdt_lo = (w_dt - wdt_hi.astype(F32)).astype(BF16)

        main, dt_raw = _inproj(x2, norm_w[l].reshape(1, D_MODEL), w_main, wdt_hi, wdt_lo, seq)

        y_ssm = _ssd(
            main, dt_raw,
            conv_w[l][:, :D_INNER], conv_b[l][:D_INNER].reshape(1, D_INNER),
            conv_w[l][:, D_INNER:], conv_b[l][D_INNER:].reshape(1, BC_DIM),
            head_col(dt_bias[l]), head_col(a_log[l]),
            jnp.repeat(d_skip[l].astype(F32), SSM_HEAD_DIM).reshape(1, D_INNER),
            ssm_norm_w[l].reshape(1, D_INNER), batch, seq)

        y_attn = _attention(
            main, rope, lambda_q1[l].reshape(1, -1), lambda_k1[l].reshape(1, -1),
            lambda_q2[l].reshape(1, -1), lambda_k2[l].reshape(1, -1),
            subln_w[l].reshape(ATTN_V_DIM, 1), batch, seq, lambda_init)

        x2 = _merge(x2, y_ssm, y_attn, main, gate_b[l].reshape(1, 2 * D_MODEL),
                    w_proj_ssm[l].astype(BF16), w_proj_attn[l].astype(BF16),
                    w_out[l].astype(BF16), final_norm_w.reshape(1, D_MODEL),
                    final_norm=(l == depth - 1))
    return x2.reshape(batch, seq, D_MODEL)
```

```python
import functools
import math

import jax
import jax.numpy as jnp
from jax import lax
from jax.experimental import pallas as pl
from jax.experimental.pallas import tpu as pltpu

F32 = jnp.float32
BF16 = jnp.bfloat16

D_MODEL = 1024
EPS = 1e-5
D_INNER = 2 * D_MODEL
SSM_HEAD_DIM = 64
SSM_HEADS = D_INNER // SSM_HEAD_DIM
SSM_GROUPS = 4
HEADS_PER_GROUP = SSM_HEADS // SSM_GROUPS
D_STATE = 128
CONV_WIDTH = 4
CHUNK = 128
BC_DIM = 2 * SSM_GROUPS * D_STATE
XBC_DIM = D_INNER + BC_DIM
ATTN_HEADS = 8
ATTN_HEAD_DIM = 64
ATTN_V_DIM = 2 * ATTN_HEAD_DIM
ATTN_WIDTH = ATTN_HEADS * ATTN_V_DIM
ROPE_THETA = 10000.0
LOG2_E = math.log2(math.e)
ATTN_BLOCK = 512

LANES = 128
BF16_SUBLANES = 16
VT_ROWS = ATTN_V_DIM + BF16_SUBLANES
PREV_ROWS = BF16_SUBLANES
CONV_SHIFT_ROWS = (CONV_WIDTH - 1) * CHUNK

COL_TILE = 1024
N_COL_TILES = 11
MAIN_DIM = COL_TILE * N_COL_TILES
Q_TILE = 5
K_TILE = 6
DT_PAD = LANES

INPROJ_ROWS = 2048
SSD_ROWS = 4 * CHUNK
VMEM_LIMIT = 56 * 1024 * 1024


def _silu(x):
    h = 0.5 * x
    return h + h * jnp.tanh(h)


def _dot(a, b):
    return jnp.dot(a, b, preferred_element_type=F32)


def _dot_nt(a, b):
    return lax.dot_general(a, b, (((1,), (1,)), ((), ())), preferred_element_type=F32)


def _inproj_kernel(x_ref, nw_ref, w_ref, wdt_hi_lo_ref, out_ref, dt_ref, h_ref):
    @pl.when(pl.program_id(1) == 0)
    def _():
        x = x_ref[...]
        var = jnp.mean(x * x, axis=-1, keepdims=True)
        h = x * lax.rsqrt(var + EPS) * nw_ref[...]
        h_hi = h.astype(BF16)
        h_ref[...] = h_hi
        h_lo = (h - h_hi.astype(F32)).astype(BF16)
        both = _dot(h_hi, wdt_hi_lo_ref[...])
        dt_ref[...] = (both[:, :DT_PAD] + both[:, DT_PAD:]) + _dot(h_lo, wdt_hi_lo_ref[:, :DT_PAD])

    out_ref[...] = _dot(h_ref[...], w_ref[...]).astype(out_ref.dtype)


def _inproj(x2, norm_w, w_main, wdt_hi_lo, seq):
    tokens = x2.shape[0]
    tm = min(INPROJ_ROWS, seq)
    return pl.pallas_call(
        _inproj_kernel,
        grid=(tokens // tm, N_COL_TILES),
        in_specs=[
            pl.BlockSpec((tm, D_MODEL), lambda i, j: (i, 0)),
            pl.BlockSpec((1, D_MODEL), lambda i, j: (0, 0)),
            pl.BlockSpec((D_MODEL, COL_TILE), lambda i, j: (0, j)),
            pl.BlockSpec((D_MODEL, 2 * DT_PAD), lambda i, j: (0, 0)),
        ],
        out_specs=[
            pl.BlockSpec((tm, COL_TILE), lambda i, j: (i, j)),
            pl.BlockSpec((tm, DT_PAD), lambda i, j: (i, 0)),
        ],
        out_shape=[
            jax.ShapeDtypeStruct((tokens, MAIN_DIM), BF16),
            jax.ShapeDtypeStruct((tokens, DT_PAD), F32),
        ],
        scratch_shapes=[pltpu.VMEM((tm, D_MODEL), BF16)],
        compiler_params=pltpu.CompilerParams(
            dimension_semantics=("parallel", "arbitrary"), vmem_limit_bytes=VMEM_LIMIT),
        name="inproj",
    )(x2, norm_w, w_main, wdt_hi_lo)


def _split3(v):
    hi = v.astype(BF16)
    r = v - hi.astype(F32)
    mid = r.astype(BF16)
    lo = (r - mid.astype(F32)).astype(BF16)
    return hi, mid, lo


def _conv_silu(u, prev, shift, w_ref, b_ref):
    shifted = _dot(shift, jnp.concatenate([prev, u], axis=0))
    out = b_ref[...] + w_ref[CONV_WIDTH - 1:CONV_WIDTH, :] * u.astype(F32)
    for k in range(1, CONV_WIDTH):
        out = out + w_ref[CONV_WIDTH - 1 - k:CONV_WIDTH - k, :] * shifted[(k - 1) * CHUNK:k * CHUNK, :]
    return _silu(out)


def _pad_rows_t(v_t):
    pad = jnp.zeros((LANES - v_t.shape[0], v_t.shape[1]), F32)
    return jnp.concatenate([v_t, pad], axis=0).T


def _split3_t(v_t):
    hi, mid, lo = _split3(v_t)
    return _pad_rows_t(jnp.concatenate(
        [hi.astype(F32), mid.astype(F32), lo.astype(F32)], axis=0)).astype(BF16)


def _ssd_kernel(xs_ref, z_ref, bc_ref, xs_prev_ref, bc_prev_ref, dt_ref, cwx_ref, cbx_ref,
                cwbc_ref, cbbc_ref, dtb_ref, alog_ref, dskip_ref, nw_ref, shift_ref, expand_ref,
                y_ref, state_ref):
    first = pl.program_id(1) == 0

    @pl.when(first)
    def _():
        state_ref[...] = jnp.zeros_like(state_ref)

    for sub in range(xs_ref.shape[0] // CHUNK):
        rows = slice(sub * CHUNK, (sub + 1) * CHUNK)
        if sub == 0:
            shift = shift_ref[jnp.where(first, 1, 0)]
            xs_prev, bc_prev = xs_prev_ref[...], bc_prev_ref[...]
        else:
            shift = shift_ref[0]
            hist = slice(sub * CHUNK - PREV_ROWS, sub * CHUNK)
            xs_prev, bc_prev = xs_ref[hist, :], bc_ref[hist, :]
        _ssd_chunk(xs_ref[rows, :], z_ref[rows, :], bc_ref[rows, :], xs_prev, bc_prev, dt_ref[rows, :],
                   shift, cwx_ref, cbx_ref, cwbc_ref, cbbc_ref, dtb_ref, alog_ref, dskip_ref, nw_ref,
                   expand_ref, y_ref.at[rows, :], state_ref)


def _ssd_chunk(xs_raw, z_raw, bc_raw, xs_prev, bc_prev, dt_raw, shift, cwx_ref, cbx_ref, cwbc_ref, cbbc_ref,
               dtb_ref, alog_ref, dskip_ref, nw_ref, expand_ref, y_ref, state_ref):
    xs = _conv_silu(xs_raw, xs_prev, shift, cwx_ref, cbx_ref)
    bc = _conv_silu(bc_raw, bc_prev, shift, cwbc_ref, cbbc_ref)

    dt_in = dt_raw.T[:SSM_HEADS] + dtb_ref[...]
    dt = jnp.maximum(dt_in, 0.0) + jnp.log1p(jnp.exp(-jnp.abs(dt_in)))
    adt = dt * (-jnp.exp(alog_ref[...]))

    row = lax.broadcasted_iota(jnp.int32, (CHUNK, CHUNK), 0)
    col = lax.broadcasted_iota(jnp.int32, (CHUNK, CHUNK), 1)
    tril = row >= col
    triu_bf = (row <= col).astype(BF16)
    hi, mid, lo = _split3(adt)
    a_cs_t = _dot(hi, triu_bf) + _dot(mid, triu_bf) + _dot(lo, triu_bf)
    key_t = a_cs_t - jnp.log(dt)
    a_cs = _pad_rows_t(a_cs_t)
    e_acs_t = jnp.exp(a_cs_t)
    w_end_t = dt * jnp.exp(a_cs_t[:, CHUNK - 1:CHUNK] - a_cs_t)
    expanded = _dot(jnp.concatenate([_split3_t(e_acs_t), _split3_t(w_end_t)], axis=0),
                    expand_ref[...])
    e_acs_full = expanded[:CHUNK]
    xs_bf = xs.astype(BF16)
    xw_bf = (xs * expanded[CHUNK:]).astype(BF16)

    gw = D_INNER // SSM_GROUPS
    pair_w = 2 * SSM_HEAD_DIM
    lane = lax.broadcasted_iota(jnp.int32, (CHUNK, pair_w), 1)
    y_parts = []
    for g in range(SSM_GROUPS):
        b_g = bc[:, g * D_STATE:(g + 1) * D_STATE]
        c_bf = bc[:, BC_DIM // 2 + g * D_STATE:BC_DIM // 2 + (g + 1) * D_STATE].astype(BF16)
        cb = _dot_nt(c_bf, b_g.astype(BF16))
        s_g = state_ref[g]
        cols = slice(g * gw, (g + 1) * gw)
        y_off = _dot(c_bf, s_g.astype(BF16)) * e_acs_full[:, cols]
        for pr in range(HEADS_PER_GROUP // 2):
            h0 = g * HEADS_PER_GROUP + 2 * pr
            x_pair = xs_bf[:, h0 * SSM_HEAD_DIM:(h0 + 2) * SSM_HEAD_DIM]
            ys = []
            for h in (h0, h0 + 1):
                seg = a_cs[:, h:h + 1] - key_t[h:h + 1, :]
                m = (cb * jnp.exp(jnp.where(tril, seg, -jnp.inf))).astype(BF16)
                ys.append(_dot(m, x_pair))
            y_parts.append(jnp.where(lane < SSM_HEAD_DIM, ys[0], ys[1])
                           + y_off[:, pr * pair_w:(pr + 1) * pair_w])
        state_ref[g] = (s_g * e_acs_full[CHUNK - 1:CHUNK, cols]
                        + _dot(b_g.T.astype(BF16), xw_bf[:, cols]))

    y = jnp.concatenate(y_parts, axis=1) + xs * dskip_ref[...]
    y = y * _silu(z_raw.astype(F32))
    for g in range(SSM_GROUPS):
        yg = y[:, g * gw:(g + 1) * gw]
        var = jnp.mean(yg * yg, axis=-1, keepdims=True)
        y_ref[:, g * gw:(g + 1) * gw] = (
            yg * lax.rsqrt(var + EPS) * nw_ref[:, g * gw:(g + 1) * gw]).astype(y_ref.dtype)


def _ssd_constants():
    t = jnp.arange(CONV_SHIFT_ROWS)[:, None]
    j = jnp.arange(CHUNK + PREV_ROWS)[None, :]
    shift = (j == PREV_ROWS + (t % CHUNK) - (t // CHUNK + 1))
    shift = jnp.stack([shift, shift & (j >= PREV_ROWS)]).astype(BF16)
    r = jnp.arange(LANES)[:, None]
    c = jnp.arange(D_INNER)[None, :]
    expand = ((r < 3 * SSM_HEADS) & (r % SSM_HEADS == c // SSM_HEAD_DIM)).astype(BF16)
    return shift, expand


def _ssd(main, dt_raw, cwx, cbx, cwbc, cbbc, dtb, alog, dskip, nw, batch, seq):
    rows = min(SSD_ROWS, seq)
    nc = seq // rows
    tokens = batch * seq
    shift, expand = _ssd_constants()
    row_idx = lambda b, c: b * nc + c
    prev_idx = lambda b, c: jnp.maximum((b * nc + c) * (rows // PREV_ROWS) - 1, 0)
    const = lambda b, c: (0, 0)
    return pl.pallas_call(
        _ssd_kernel,
        grid=(batch, nc),
        in_specs=[
            pl.BlockSpec((rows, D_INNER), lambda b, c: (row_idx(b, c), 0)),
            pl.BlockSpec((rows, D_INNER), lambda b, c: (row_idx(b, c), 1)),
            pl.BlockSpec((rows, BC_DIM), lambda b, c: (row_idx(b, c), 4)),
            pl.BlockSpec((PREV_ROWS, D_INNER), lambda b, c: (prev_idx(b, c), 0)),
            pl.BlockSpec((PREV_ROWS, BC_DIM), lambda b, c: (prev_idx(b, c), 4)),
            pl.BlockSpec((rows, DT_PAD), lambda b, c: (row_idx(b, c), 0)),
            pl.BlockSpec((CONV_WIDTH, D_INNER), const),
            pl.BlockSpec((1, D_INNER), const),
            pl.BlockSpec((CONV_WIDTH, BC_DIM), const),
            pl.BlockSpec((1, BC_DIM), const),
            pl.BlockSpec((SSM_HEADS, 1), const),
            pl.BlockSpec((SSM_HEADS, 1), const),
            pl.BlockSpec((1, D_INNER), const),
            pl.BlockSpec((1, D_INNER), const),
            pl.BlockSpec((2, CONV_SHIFT_ROWS, CHUNK + PREV_ROWS), lambda b, c: (0, 0, 0)),
            pl.BlockSpec((LANES, D_INNER), const),
        ],
        out_specs=pl.BlockSpec((rows, D_INNER), lambda b, c: (row_idx(b, c), 0)),
        out_shape=jax.ShapeDtypeStruct((tokens, D_INNER), BF16),
        scratch_shapes=[
            pltpu.VMEM((SSM_GROUPS, D_STATE, D_INNER // SSM_GROUPS), F32),
        ],
        compiler_params=pltpu.CompilerParams(
            dimension_semantics=("parallel", "arbitrary"), vmem_limit_bytes=VMEM_LIMIT),
        name="ssd",
    )(main, main, main, main, main, dt_raw, cwx, cbx, cwbc, cbbc, dtb, alog, dskip, nw, shift, expand)


def _rotate_half_rows(x_t):
    half = ATTN_HEAD_DIM // 2
    parts = []
    for m in range(2):
        base = m * ATTN_HEAD_DIM
        parts += [x_t[base + half:base + 2 * half], x_t[base:base + half]]
    return jnp.concatenate(parts, axis=0)


def _rotate_half_lanes(x):
    half = ATTN_HEAD_DIM // 2
    lane = lax.broadcasted_iota(jnp.int32, x.shape, 1)
    return jnp.where((lane % ATTN_HEAD_DIM) < half,
                     pltpu.roll(x, LANES - half, 1), pltpu.roll(x, half, 1))


def _attn_kernel(q_ref, k_ref, v_ref, z_ref, cosq_ref, sinq_ref, cosk_ref, sink_ref,
                 lq1_ref, lk1_ref, lq2_ref, lk2_ref, sw_ref,
                 o_ref, vt_ref, kr_ref, *scratch, tq, tk, lambda_init):
    ones_rows = (lax.broadcasted_iota(jnp.int32, (VT_ROWS - ATTN_V_DIM, tk), 0) == 0).astype(BF16)
    for c in range(vt_ref.shape[0]):
        rows = slice(c * tk, (c + 1) * tk)
        vt_ref[c, :ATTN_V_DIM, :] = v_ref[rows, :].astype(F32).T.astype(BF16)
        vt_ref[c, ATTN_V_DIM:, :] = ones_rows
        k = k_ref[rows, :].astype(F32)
        kr_ref[rows, :] = (k * cosk_ref[rows, :]
                           + _rotate_half_lanes(k) * sink_ref[rows, :]).astype(BF16)

    def qblock(qi, carry):
        _attn_qblock(qi, q_ref, z_ref, cosq_ref, sinq_ref, lq1_ref, lk1_ref, lq2_ref, lk2_ref, sw_ref,
                     o_ref, vt_ref, kr_ref, *scratch, tq=tq, tk=tk, lambda_init=lambda_init)
        return carry

    lax.fori_loop(0, q_ref.shape[0] // tq, qblock, 0)


def _attn_qblock(qi, q_ref, z_ref, cosq_ref, sinq_ref, lq1_ref, lk1_ref, lq2_ref, lk2_ref, sw_ref,
                 o_ref, vt_ref, kr_ref, qq_ref, s0_ref, s1_ref, mb0_ref, mb1_ref, p0_ref, p1_ref,
                 a0_ref, a1_ref, m_ref, acc_ref, *, tq, tk, lambda_init):
    q_rows = pl.ds(pl.multiple_of(qi * tq, tq), tq)
    q_t = q_ref[q_rows, :].astype(F32).T
    q_t = q_t * cosq_ref[qi] + _rotate_half_rows(q_t) * sinq_ref[qi]
    row = lax.broadcasted_iota(jnp.int32, q_t.shape, 0)
    qq_ref[...] = jnp.concatenate([jnp.where(row < ATTN_HEAD_DIM, q_t, 0.0),
                                   jnp.where(row >= ATTN_HEAD_DIM, q_t, 0.0)], axis=1).astype(BF16)

    m_ref[...] = jnp.full_like(m_ref, -jnp.inf)
    acc_ref[...] = jnp.zeros_like(acc_ref)
    p1_ref[...] = jnp.zeros_like(p1_ref)
    a1_ref[...] = jnp.ones_like(a1_ref)

    def scores(j, s_ref, mb_ref):
        start = j * tk if isinstance(j, int) else pl.multiple_of(j * tk, tk)
        s = _dot(kr_ref[pl.ds(start, tk), :], qq_ref[...])
        s_ref[...] = s
        mb_ref[...] = jnp.max(s, axis=0, keepdims=True)

    def softmax(s_ref, mb_ref, p_ref, a_ref, diag_half=None):
        s = s_ref[...]
        if diag_half is None:
            mb = mb_ref[...]
        else:
            key = lax.broadcasted_iota(jnp.int32, s.shape, 0) + diag_half * tk
            qry = lax.broadcasted_iota(jnp.int32, s.shape, 1) & (tq - 1)
            s = jnp.where(key <= qry, s, -jnp.inf)
            mb = jnp.max(s, axis=0, keepdims=True)
        m_prev = m_ref[...]
        m_new = jnp.maximum(m_prev, mb)
        a_ref[...] = jnp.exp2(m_prev - m_new)
        p_ref[...] = jnp.exp2((s - m_new).astype(BF16))
        m_ref[...] = m_new

    def value(j, p_ref, a_ref):
        acc_ref[...] = acc_ref[...] * a_ref[...] + _dot(vt_ref[jnp.maximum(j, 0)], p_ref[...])

    scores(0, s0_ref, mb0_ref)

    def body(k, carry):
        softmax(s0_ref, mb0_ref, p0_ref, a0_ref)
        value(2 * k - 1, p1_ref, a1_ref)
        scores(2 * k + 1, s1_ref, mb1_ref)
        softmax(s1_ref, mb1_ref, p1_ref, a1_ref)
        value(2 * k, p0_ref, a0_ref)
        scores(2 * k + 2, s0_ref, mb0_ref)
        return carry

    lax.fori_loop(0, qi, body, 0)
    scores(2 * qi + 1, s1_ref, mb1_ref)
    softmax(s0_ref, mb0_ref, p0_ref, a0_ref, diag_half=0)
    value(2 * qi - 1, p1_ref, a1_ref)
    softmax(s1_ref, mb1_ref, p1_ref, a1_ref, diag_half=1)
    value(2 * qi, p0_ref, a0_ref)
    value(2 * qi + 1, p1_ref, a1_ref)

    lam = (jnp.exp(jnp.sum(lq1_ref[...] * lk1_ref[...], axis=-1, keepdims=True))
           - jnp.exp(jnp.sum(lq2_ref[...] * lk2_ref[...], axis=-1, keepdims=True))
           + lambda_init)
    o_maps = acc_ref[:ATTN_V_DIM, :] * (1.0 / acc_ref[ATTN_V_DIM:ATTN_V_DIM + 1, :])
    o = o_maps[:, :tq] - lam * o_maps[:, tq:]
    var = jnp.mean(o * o, axis=0, keepdims=True)
    o = o * lax.rsqrt(var + EPS) * sw_ref[...] * (1.0 - lambda_init)
    o_ref[q_rows, :] = (o.T * _silu(z_ref[q_rows, :].astype(F32))).astype(o_ref.dtype)


def _attention(main, rope, lq1, lk1, lq2, lk2, subln_w, batch, seq, lambda_init):
    tq = min(ATTN_BLOCK, seq)
    tk = tq // 2
    assert tq & (tq - 1) == 0 and seq % tq == 0
    nq = seq // tq
    tokens = batch * seq
    q_col = Q_TILE * COL_TILE // ATTN_V_DIM
    k_col = K_TILE * COL_TILE // ATTN_V_DIM
    v_col = (K_TILE + 1) * COL_TILE // ATTN_V_DIM
    z_col = (K_TILE + 2) * COL_TILE // ATTN_V_DIM
    const = lambda b, h: (0, 0)
    cos_q, sin_q, cos_k, sin_k = rope
    cos_q = cos_q.reshape(ATTN_V_DIM, nq, tq).transpose(1, 0, 2)
    sin_q = sin_q.reshape(ATTN_V_DIM, nq, tq).transpose(1, 0, 2)
    return pl.pallas_call(
        functools.partial(_attn_kernel, tq=tq, tk=tk, lambda_init=lambda_init),
        grid=(batch, ATTN_HEADS),
        in_specs=[
            pl.BlockSpec((seq, ATTN_V_DIM), lambda b, h: (b, q_col + h)),
            pl.BlockSpec((seq, ATTN_V_DIM), lambda b, h: (b, k_col + h)),
            pl.BlockSpec((seq, ATTN_V_DIM), lambda b, h: (b, v_col + h)),
            pl.BlockSpec((seq, ATTN_V_DIM), lambda b, h: (b, z_col + h)),
            pl.BlockSpec((nq, ATTN_V_DIM, tq), lambda b, h: (0, 0, 0)),
            pl.BlockSpec((nq, ATTN_V_DIM, tq), lambda b, h: (0, 0, 0)),
            pl.BlockSpec((seq, ATTN_V_DIM), const),
            pl.BlockSpec((seq, ATTN_V_DIM), const),
            pl.BlockSpec((1, ATTN_HEAD_DIM), const),
            pl.BlockSpec((1, ATTN_HEAD_DIM), const),
            pl.BlockSpec((1, ATTN_HEAD_DIM), const),
            pl.BlockSpec((1, ATTN_HEAD_DIM), const),
            pl.BlockSpec((ATTN_V_DIM, 1), const),
        ],
        out_specs=pl.BlockSpec((seq, ATTN_V_DIM), lambda b, h: (b, h)),
        out_shape=jax.ShapeDtypeStruct((tokens, ATTN_WIDTH), BF16),
        scratch_shapes=[
            pltpu.VMEM((seq // tk, VT_ROWS, tk), BF16),
            pltpu.VMEM((seq, ATTN_V_DIM), BF16),
            pltpu.VMEM((ATTN_V_DIM, 2 * tq), BF16),
            pltpu.VMEM((tk, 2 * tq), F32),
            pltpu.VMEM((tk, 2 * tq), F32),
            pltpu.VMEM((1, 2 * tq), F32),
            pltpu.VMEM((1, 2 * tq), F32),
            pltpu.VMEM((tk, 2 * tq), BF16),
            pltpu.VMEM((tk, 2 * tq), BF16),
            pltpu.VMEM((1, 2 * tq), F32),
            pltpu.VMEM((1, 2 * tq), F32),
            pltpu.VMEM((1, 2 * tq), F32),
            pltpu.VMEM((VT_ROWS, 2 * tq), F32),
        ],
        compiler_params=pltpu.CompilerParams(
            dimension_semantics=("parallel", "parallel"),
            vmem_limit_bytes=VMEM_LIMIT),
        name="diff_attn",
    )(main, main, main, main, cos_q, sin_q, cos_k, sin_k, lq1, lk1, lq2, lk2, subln_w)


def _merge_kernel(x_ref, ys_ref, ya_ref, gs_ref, ga_ref, gb_ref, ps_ref, pa_ref, wo_ref, fw_ref,
                  o_ref, *, final_norm):
    proj_s = _dot(ys_ref[...], ps_ref[...])
    proj_a = _dot(ya_ref[...], pa_ref[...])
    g_s = 1.0 / (1.0 + jnp.exp(-(gs_ref[...].astype(F32) + gb_ref[:, :D_MODEL])))
    g_a = 1.0 / (1.0 + jnp.exp(-(ga_ref[...].astype(F32) + gb_ref[:, D_MODEL:])))
    merged = g_s * proj_s + g_a * proj_a
    x = x_ref[...] + _dot(merged.astype(BF16), wo_ref[...])
    if final_norm:
        var = jnp.mean(x * x, axis=-1, keepdims=True)
        x = x * lax.rsqrt(var + EPS) * fw_ref[...]
    o_ref[...] = x


def _merge(x2, y_ssm, y_attn, main, gate_b, p_ssm, p_attn, w_out, final_w, final_norm):
    tokens = x2.shape[0]
    tm = min(512, tokens)
    const = lambda i: (0, 0)
    return pl.pallas_call(
        functools.partial(_merge_kernel, final_norm=final_norm),
        grid=(tokens // tm,),
        in_specs=[
            pl.BlockSpec((tm, D_MODEL), lambda i: (i, 0)),
            pl.BlockSpec((tm, D_INNER), lambda i: (i, 0)),
            pl.BlockSpec((tm, ATTN_WIDTH), lambda i: (i, 0)),
            pl.BlockSpec((tm, D_MODEL), lambda i: (i, N_COL_TILES - 2)),
            pl.BlockSpec((tm, D_MODEL), lambda i: (i, N_COL_TILES - 1)),
            pl.BlockSpec((1, 2 * D_MODEL), const),
            pl.BlockSpec((D_INNER, D_MODEL), const),
            pl.BlockSpec((ATTN_WIDTH, D_MODEL), const),
            pl.BlockSpec((D_MODEL, D_MODEL), const),
            pl.BlockSpec((1, D_MODEL), const),
        ],
        out_specs=pl.BlockSpec((tm, D_MODEL), lambda i: (i, 0)),
        out_shape=jax.ShapeDtypeStruct((tokens, D_MODEL), F32),
        compiler_params=pltpu.CompilerParams(
            dimension_semantics=("parallel",), vmem_limit_bytes=VMEM_LIMIT),
        name="merge",
    )(x2, y_ssm, y_attn, main, main, gate_b, p_ssm, p_attn, w_out, final_w)


def _rope_tables(seq):
    inv = 1.0 / (ROPE_THETA ** (jnp.arange(0, ATTN_HEAD_DIM, 2, dtype=F32) / ATTN_HEAD_DIM))
    f = jnp.arange(seq, dtype=F32)[:, None] * inv[None, :]
    cos = jnp.cos(f)
    sin = jnp.sin(f)
    cos_k = jnp.concatenate([cos, cos, cos, cos], axis=-1)
    sin_k = jnp.concatenate([-sin, sin, -sin, sin], axis=-1)
    q_scale = ATTN_HEAD_DIM ** -0.5 * LOG2_E
    return (cos_k.T * q_scale, sin_k.T * q_scale, cos_k, sin_k)


def kernel(x, norm_w, w_in, conv_w, conv_b, dt_bias, a_log, d_skip, ssm_norm_w,
           lambda_q1, lambda_k1, lambda_q2, lambda_k2, subln_w, gate_b,
           w_proj_ssm, w_proj_attn, w_out, final_norm_w):
    batch, seq, _ = x.shape
    depth = norm_w.shape[0]
    assert seq % CHUNK == 0
    rope = _rope_tables(seq)
    x2 = x.reshape(batch * seq, D_MODEL)
    head_col = lambda v: v.astype(F32).reshape(SSM_HEADS, 1)

    dt_off = D_INNER + XBC_DIM
    for l in range(depth):
        lambda_init = 0.8 - 0.6 * math.exp(-0.3 * l)
        w = w_in[l]
        w_main = jnp.concatenate(
            [w[:, D_INNER:2 * D_INNER], w[:, :D_INNER], w[:, 2 * D_INNER:dt_off],
             w[:, dt_off + SSM_HEADS:]], axis=1).astype(BF16)
        w_dt = jnp.pad(w[:, dt_off:dt_off + SSM_HEADS], ((0, 0), (0, DT_PAD - SSM_HEADS)))
        wdt_hi = w_dt.astype(BF16)
        wdt_lo = (w_dt - wdt_hi.astype(F32)).astype(BF16)
        wdt_hi_lo = jnp.concatenate([wdt_hi, wdt_lo], axis=1)

        main, dt_raw = _inproj(x2, norm_w[l].reshape(1, D_MODEL), w_main, wdt_hi_lo, seq)

        y_ssm = _ssd(
            main, dt_raw,
            conv_w[l][:, :D_INNER], conv_b[l][:D_INNER].reshape(1, D_INNER),
            conv_w[l][:, D_INNER:], conv_b[l][D_INNER:].reshape(1, BC_DIM),
            head_col(dt_bias[l]), head_col(a_log[l]),
            jnp.repeat(d_skip[l].astype(F32), SSM_HEAD_DIM).reshape(1, D_INNER),
            ssm_norm_w[l].reshape(1, D_INNER), batch, seq)

        y_attn = _attention(
            main, rope, lambda_q1[l].reshape(1, -1), lambda_k1[l].reshape(1, -1),
            lambda_q2[l].reshape(1, -1), lambda_k2[l].reshape(1, -1),
            subln_w[l].reshape(ATTN_V_DIM, 1), batch, seq, lambda_init)

        x2 = _merge(x2, y_ssm, y_attn, main, gate_b[l].reshape(1, 2 * D_MODEL),
                    w_proj_ssm[l].astype(BF16), w_proj_attn[l].astype(BF16),
                    w_out[l].astype(BF16), final_norm_w.reshape(1, D_MODEL),
                    final_norm=(l == depth - 1))
    return x2.reshape(batch, seq, D_MODEL)
```
